```python
import math
import jax
import jax.numpy as jnp
from jax import lax
import numpy as np

D_MODEL = 2048
BATCH = 8
SEQ = 2048
DEPTH = 2

GRID_W = 64
CTX_LEN = 256

GDN_HEADS = 16
GDN_DK = 128
GDN_DV = 128
CONV_K = 5
CHUNK = 64
GDN_QK_W = GDN_HEADS * GDN_DK
GDN_V_W = GDN_HEADS * GDN_DV
QKV_W = 2 * GDN_QK_W + GDN_V_W

MLA_HEADS = 16
Q_LORA = 512
KV_LORA = 512
NOPE_DIM = 128
ROPE_DIM = 64
V_DIM = 128
ROPE_THETA = 10000.0
Q_BLOCK = 128
MLA_SCALE = (NOPE_DIM + ROPE_DIM) ** -0.5

FF_DENSE = 5632
N_EXPERTS = 8
TOP_K = 2
FF_EXPERT = 7168
N_DENSE = (DEPTH + 1) // 2
N_MOE = DEPTH // 2

DEEPNORM_ALPHA = (2 * DEPTH) ** 0.25
DEEPNORM_BETA = (8 * DEPTH) ** -0.25
EPS = 1e-6

PROJ_SIZES = (QKV_W, GDN_V_W, 2 * GDN_HEADS, 2 * GDN_HEADS, Q_LORA, KV_LORA, ROPE_DIM, D_MODEL, D_MODEL)
PROJ_W = sum(PROJ_SIZES)

kernel_name = 'hybrid_gdn_mla_moe_diffusion_trunk'


def layer_norm(x, g=None, b=None):
    xf = x.astype(jnp.float32)
    xc = xf - jnp.mean(xf, -1, keepdims=True)
    y = xc * lax.rsqrt(jnp.mean(xc * xc, -1, keepdims=True) + EPS)
    if g is not None:
        y = y * g.astype(jnp.float32) + b.astype(jnp.float32)
    return y.astype(x.dtype)


def rms_norm(x, g):
    xf = x.astype(jnp.float32)
    y = xf * lax.rsqrt(jnp.mean(xf * xf, -1, keepdims=True) + EPS) * g.astype(jnp.float32)
    return y.astype(x.dtype)


def l2_normalize(x):
    xf = x.astype(jnp.float32)
    return xf * lax.rsqrt(jnp.sum(xf * xf, -1, keepdims=True) + EPS)


def modulate(x, shift, scale):
    return layer_norm(x) * (1.0 + scale) + shift


def split_proj(p):
    cuts, acc = [], 0
    for s in PROJ_SIZES[:-1]:
        acc += s
        cuts.append(acc)
    return jnp.split(p, cuts, axis=-1)


def centred_dwconv(x, w):
    return lax.conv_general_dilated(
        x, w[:, None, :].astype(x.dtype), window_strides=(1,),
        padding=[(CONV_K // 2, CONV_K // 2)], dimension_numbers=('NWC', 'WIO', 'NWC'),
        feature_group_count=x.shape[-1])


def gdn_prep(qkv, a, b, conv_w, a_log, dt_bias):
    bsz, n = qkv.shape[:2]
    qkv = jax.nn.silu(centred_dwconv(qkv, conv_w))
    q, k, v = jnp.split(qkv, [GDN_QK_W, 2 * GDN_QK_W], axis=-1)
    q = l2_normalize(q.reshape(bsz, n, GDN_HEADS, GDN_DK)) * (GDN_DK ** -0.5)
    k = l2_normalize(k.reshape(bsz, n, GDN_HEADS, GDN_DK))
    v = v.reshape(bsz, n, GDN_HEADS, GDN_DV)
    a = a.astype(jnp.float32).reshape(bsz, n, 2, GDN_HEADS)
    b = b.astype(jnp.float32).reshape(bsz, n, 2, GDN_HEADS)
    g = -jnp.exp(a_log.astype(jnp.float32)) * jax.nn.softplus(a + dt_bias.astype(jnp.float32))
    beta = jax.nn.sigmoid(b)
    return q, k, v, g, beta


def unit_lower_inverse(l):
    eye = jnp.eye(CHUNK, dtype=l.dtype)
    p = -l
    inv = eye + p
    for _ in range(CHUNK.bit_length() - 2):
        p = p @ p
        inv = inv @ (eye + p)
    return inv


def delta_rule_chunked(q, k, v, g, beta, s0):
    bsz, n, h = q.shape[:3]
    nc = n // CHUNK
    f32 = jnp.float32

    def to_chunks(t):
        t = t.astype(f32).reshape((bsz, nc, CHUNK, h) + t.shape[3:])
        return t.transpose((1, 0, 3, 2) + tuple(range(4, t.ndim)))

    qc, kc, vc, gc, bc = (to_chunks(t) for t in (q, k, v, g, beta))
    gcum = jnp.cumsum(gc, axis=-1)
    glast = gcum[..., -1]
    incl = jnp.tril(jnp.ones((CHUNK, CHUNK), bool))
    strict = jnp.tril(jnp.ones((CHUNK, CHUNK), bool), -1)
    decay = jnp.exp(jnp.where(incl, gcum[..., :, None] - gcum[..., None, :], -jnp.inf))
    kbeta = kc * bc[..., None]
    lmat = jnp.where(strict, jnp.einsum('nbhik,nbhjk->nbhij', kbeta, kc) * decay, 0.0)
    t_inv = unit_lower_inverse(lmat)
    u = jnp.einsum('nbhij,nbhjv->nbhiv', t_inv, vc * bc[..., None])
    w = jnp.einsum('nbhij,nbhjk->nbhik', t_inv, kbeta * jnp.exp(gcum)[..., None])
    a_intra = jnp.einsum('nbhik,nbhjk->nbhij', qc, kc) * decay
    q_dec = qc * jnp.exp(gcum)[..., None]
    k_dec = kc * jnp.exp(glast[..., None] - gcum)[..., None]

    def step(state, xs):
        q_i, k_i, u_i, w_i, a_i, gl_i = xs
        v_new = u_i - jnp.einsum('bhck,bhkv->bhcv', w_i, state)
        o_i = jnp.einsum('bhck,bhkv->bhcv', q_i, state) + jnp.einsum('bhij,bhjv->bhiv', a_i, v_new)
        state = state * jnp.exp(gl_i)[..., None, None] + jnp.einsum('bhck,bhcv->bhkv', k_i, v_new)
        return state, o_i

    s_final, o = lax.scan(step, s0.astype(f32), (q_dec, k_dec, u, w, a_intra, glast))
    o = o.transpose(1, 0, 3, 2, 4).reshape(bsz, n, h, -1)
    return o, s_final


def gdn_direction(inputs, d, s0, reverse):
    q, k, v, g, beta = inputs
    g, beta = g[:, :, d], beta[:, :, d]
    if reverse:
        q, k, v, g, beta = (jnp.flip(t, axis=1) for t in (q, k, v, g, beta))
    o, s = delta_rule_chunked(q, k, v, g, beta, s0)
    if reverse:
        o = jnp.flip(o, axis=1)
    return o, s


def gdn_output(o, z, norm_g):
    bsz, n = z.shape[:2]
    y = rms_norm(o, norm_g) * jax.nn.silu(z.astype(jnp.float32).reshape(bsz, n, GDN_HEADS, GDN_DV))
    return y.reshape(bsz, n, GDN_V_W).astype(z.dtype)


def axial_rope(x, cos, sin):
    shp = x.shape
    xr = x.reshape(shp[:-1] + (2, 2, ROPE_DIM // 4))
    x1, x2 = xr[..., 0, :], xr[..., 1, :]
    out = jnp.stack([x1 * cos - x2 * sin, x2 * cos + x1 * sin], axis=-2)
    return out.reshape(shp)


def mla_prep(dq, dkv, kr, q_norm, kv_norm, w_uq, w_ukv, cos, sin):
    bsz, n = dq.shape[:2]
    q = (rms_norm(dq, q_norm) @ w_uq).reshape(bsz, n, MLA_HEADS, NOPE_DIM + ROPE_DIM)
    kv = (rms_norm(dkv, kv_norm) @ w_ukv).reshape(bsz, n, MLA_HEADS, NOPE_DIM + V_DIM)
    q_nope, q_rope = q[..., :NOPE_DIM], q[..., NOPE_DIM:]
    k_nope, v = kv[..., :NOPE_DIM], kv[..., NOPE_DIM:]
    k_rope = kr
    if cos is not None:
        q_rope = axial_rope(q_rope, cos[:, None], sin[:, None])
        k_rope = axial_rope(k_rope, cos, sin)
    return q_nope, q_rope, k_nope, k_rope, v


def mla_attend(q_nope, q_rope, k_nope, k_rope, v):
    s = (jnp.einsum('bqhd,bkhd->bhqk', q_nope, k_nope, preferred_element_type=jnp.float32)
         + jnp.einsum('bqhr,bkr->bhqk', q_rope, k_rope, preferred_element_type=jnp.float32)) * MLA_SCALE
    p = jax.nn.softmax(s, axis=-1).astype(v.dtype)
    return jnp.einsum('bhqk,bkhd->bqhd', p, v)


def mla_latent(q_nope, q_rope, k_nope, k_rope, v):
    bsz, n = q_nope.shape[:2]

    def blocks(t):
        return jnp.moveaxis(t.reshape((bsz, n // Q_BLOCK, Q_BLOCK) + t.shape[2:]), 1, 0)

    o = lax.map(lambda qb: mla_attend(qb[0], qb[1], k_nope, k_rope, v), (blocks(q_nope), blocks(q_rope)))
    return jnp.moveaxis(o, 0, 1).reshape(bsz, n, MLA_HEADS * V_DIM)


def merge_out(y_gdn, y_mla, gate_a, gate_b, w_br_a, w_br_b, w_out):
    y = jax.nn.sigmoid(gate_a) * (y_gdn @ w_br_a) + jax.nn.sigmoid(gate_b) * (y_mla @ w_br_b)
    return y @ w_out


def swiglu(h, w1, w3, w2):
    return (jax.nn.silu(h @ w1) * (h @ w3)) @ w2


def moe_swiglu(h, router, w1, w3, w2):
    shp = h.shape
    t = h.reshape(-1, shp[-1])
    logits = jnp.dot(t, router, preferred_element_type=jnp.float32)
    top_v, top_i = lax.top_k(logits, TOP_K)
    top_w = jax.nn.softmax(top_v, axis=-1)
    combine = jnp.sum(jax.nn.one_hot(top_i, N_EXPERTS, dtype=jnp.float32) * top_w[..., None], axis=1)
    y = jnp.zeros_like(t)
    for e in range(N_EXPERTS):
        y = y + combine[:, e:e + 1].astype(t.dtype) * swiglu(t, w1[e], w3[e], w2[e])
    return y.reshape(shp)


def setup_inputs(seed: int = 0) -> dict:
    key = jax.random.key(seed)
    ks = iter(jax.random.split(key, 40))
    f32 = jnp.float32
    D = D_MODEL

    def nrm(shape, fan_in, gain=1.0):
        return jax.random.normal(next(ks), shape, f32) * (gain * fan_in ** -0.5)

    def gain(shape):
        return 1.0 + 0.02 * jax.random.normal(next(ks), shape, f32)

    def small(shape):
        return 0.02 * jax.random.normal(next(ks), shape, f32)

    x = jax.random.normal(next(ks), (BATCH, SEQ, D), f32)
    c = jax.random.normal(next(ks), (BATCH, D), f32)
    ctx = jax.random.normal(next(ks), (BATCH, CTX_LEN, D), f32)
    c_ctx = jax.random.normal(next(ks), (D,), f32)
    w_mod = nrm((DEPTH, D, 6 * D), D, 0.5)
    b_mod = small((DEPTH, 6 * D))
    w_in = nrm((DEPTH, D, PROJ_W), D)
    conv_w = nrm((DEPTH, CONV_K, QKV_W), CONV_K)
    a_log = jnp.log(jax.random.uniform(next(ks), (DEPTH, 2, GDN_HEADS), f32, 1.0, 16.0))
    dt = jnp.exp(jax.random.uniform(next(ks), (DEPTH, 2, GDN_HEADS), f32, math.log(1e-3), math.log(1e-1)))
    dt_bias = dt + jnp.log(-jnp.expm1(-dt))
    gdn_norm = gain((DEPTH, GDN_DV))
    q_norm = gain((DEPTH, Q_LORA))
    kv_norm = gain((DEPTH, KV_LORA))
    w_uq = nrm((DEPTH, Q_LORA, MLA_HEADS * (NOPE_DIM + ROPE_DIM)), Q_LORA)
    w_ukv = nrm((DEPTH, KV_LORA, MLA_HEADS * (NOPE_DIM + V_DIM)), KV_LORA)
    w_br_a = nrm((DEPTH, GDN_V_W, D), GDN_V_W)
    w_br_b = nrm((DEPTH, MLA_HEADS * V_DIM, D), MLA_HEADS * V_DIM)
    w_out = nrm((DEPTH, D, D), D, DEEPNORM_BETA)
    ln1_g = gain((DEPTH, D))
    ln1_b = small((DEPTH, D))
    ln2_g = gain((DEPTH, D))
    ln2_b = small((DEPTH, D))
    ffn_w1 = nrm((N_DENSE, D, FF_DENSE), D)
    ffn_w3 = nrm((N_DENSE, D, FF_DENSE), D)
    ffn_w2 = nrm((N_DENSE, FF_DENSE, D), FF_DENSE, DEEPNORM_BETA)
    moe_router = nrm((N_MOE, D, N_EXPERTS), D)
    moe_w1 = nrm((N_MOE, N_EXPERTS, D, FF_EXPERT), D)
    moe_w3 = nrm((N_MOE, N_EXPERTS, D, FF_EXPERT), D)
    moe_w2 = nrm((N_MOE, N_EXPERTS, FF_EXPERT, D), FF_EXPERT, DEEPNORM_BETA)
    return {'x': x, 'c': c, 'ctx': ctx, 'c_ctx': c_ctx, 'w_mod': w_mod, 'b_mod': b_mod, 'w_in': w_in,
            'conv_w': conv_w, 'a_log': a_log, 'dt_bias': dt_bias, 'gdn_norm': gdn_norm, 'q_norm': q_norm,
            'kv_norm': kv_norm, 'w_uq': w_uq, 'w_ukv': w_ukv, 'w_br_a': w_br_a, 'w_br_b': w_br_b,
            'w_out': w_out, 'ln1_g': ln1_g, 'ln1_b': ln1_b, 'ln2_g': ln2_g, 'ln2_b': ln2_b,
            'ffn_w1': ffn_w1, 'ffn_w3': ffn_w3, 'ffn_w2': ffn_w2, 'moe_router': moe_router,
            'moe_w1': moe_w1, 'moe_w3': moe_w3, 'moe_w2': moe_w2}


def reference(x, c, ctx, c_ctx, w_mod, b_mod, w_in, conv_w, a_log, dt_bias, gdn_norm, q_norm, kv_norm,
              w_uq, w_ukv, w_br_a, w_br_b, w_out, ln1_g, ln1_b, ln2_g, ln2_b, ffn_w1, ffn_w3, ffn_w2,
              moe_router, moe_w1, moe_w3, moe_w2):
    bsz, n_lat, _ = x.shape
    rows = n_lat // GRID_W
    f32 = jnp.float32
    row = jnp.repeat(jnp.arange(rows), GRID_W).astype(f32)
    col = jnp.tile(jnp.arange(GRID_W), rows).astype(f32)
    n_freq = ROPE_DIM // 4
    inv_freq = ROPE_THETA ** (-jnp.arange(n_freq, dtype=f32) / n_freq)
    ang = jnp.stack([row[:, None] * inv_freq, col[:, None] * inv_freq], axis=1)
    cos, sin = jnp.cos(ang).astype(x.dtype), jnp.sin(ang).astype(x.dtype)
    s0 = jnp.zeros((bsz, GDN_HEADS, GDN_DK, GDN_DV), f32)

    for i in range(DEPTH):
        last = i == DEPTH - 1
        sh1, sc1, gt1, sh2, sc2, gt2 = jnp.split((jax.nn.silu(c) @ w_mod[i] + b_mod[i])[:, None, :], 6, axis=-1)
        csh1, csc1, cgt1, csh2, csc2, cgt2 = jnp.split(jax.nn.silu(c_ctx) @ w_mod[i] + b_mod[i], 6, axis=-1)

        pl = split_proj(modulate(x, sh1, sc1) @ w_in[i])
        pc = split_proj(modulate(ctx, csh1, csc1) @ w_in[i])

        gin_c = gdn_prep(pc[0], pc[2], pc[3], conv_w[i], a_log[i], dt_bias[i])
        gin_l = gdn_prep(pl[0], pl[2], pl[3], conv_w[i], a_log[i], dt_bias[i])
        oc_f, st_f = gdn_direction(gin_c, 0, s0, False)
        oc_b, st_b = gdn_direction(gin_c, 1, s0, True)
        ol_f, _ = gdn_direction(gin_l, 0, st_f, False)
        ol_b, _ = gdn_direction(gin_l, 1, st_b, True)
        ya_lat = gdn_output(ol_f + ol_b, pl[1], gdn_norm[i])

        ml = mla_prep(pl[4], pl[5], pl[6], q_norm[i], kv_norm[i], w_uq[i], w_ukv[i], cos, sin)
        mc = mla_prep(pc[4], pc[5], pc[6], q_norm[i], kv_norm[i], w_uq[i], w_ukv[i], None, None)
        k_nope = jnp.concatenate([ml[2], mc[2]], axis=1)
        k_rope = jnp.concatenate([ml[3], mc[3]], axis=1)
        v_all = jnp.concatenate([ml[4], mc[4]], axis=1)
        yb_lat = mla_latent(ml[0], ml[1], k_nope, k_rope, v_all)

        m_lat = merge_out(ya_lat, yb_lat, pl[7], pl[8], w_br_a[i], w_br_b[i], w_out[i])
        x_new = layer_norm(DEEPNORM_ALPHA * x + gt1 * m_lat, ln1_g[i], ln1_b[i])
        if not last:
            ya_ctx = gdn_output(oc_f + oc_b, pc[1], gdn_norm[i])
            yb_ctx = mla_attend(mc[0], mc[1], mc[2], mc[3], mc[4]).reshape(bsz, -1, MLA_HEADS * V_DIM)
            m_ctx = merge_out(ya_ctx, yb_ctx, pc[7], pc[8], w_br_a[i], w_br_b[i], w_out[i])
            ctx = layer_norm(DEEPNORM_ALPHA * ctx + cgt1 * m_ctx, ln1_g[i], ln1_b[i])
        x = x_new

        j = i // 2
        if i % 2 == 0:
            ffn = lambda h: swiglu(h, ffn_w1[j], ffn_w3[j], ffn_w2[j])
        else:
            ffn = lambda h: moe_swiglu(h, moe_router[j], moe_w1[j], moe_w3[j], moe_w2[j])
        x = layer_norm(DEEPNORM_ALPHA * x + gt2 * ffn(modulate(x, sh2, sc2)), ln2_g[i], ln2_b[i])
        if not last:
            ctx = layer_norm(DEEPNORM_ALPHA * ctx + cgt2 * ffn(modulate(ctx, csh2, csc2)), ln2_g[i], ln2_b[i])
    return x
```

```python
import functools
import math

import numpy as np
import jax
import jax.numpy as jnp
from jax import lax
from jax.experimental import pallas as pl
from jax.experimental.pallas import tpu as pltpu

F32 = jnp.float32
BF16 = jnp.bfloat16
HI = lax.Precision.HIGHEST

HEADS = 16
HEAD_DIM = 128
ROPE_DIM = 64
Q_LORA = 512
KV_LORA = 512
CHUNK = 64
CONV_K = 5
GRID_W = 64
ROPE_THETA = 10000.0
QK_W = HEADS * HEAD_DIM
MLA_SCALE = (HEAD_DIM + ROPE_DIM) ** -0.5
TOP_K = 2
EPS = 1e-6
ROW_TILE = 256
MOE_TILE = 512
LANES = 128


def _cparams(sem, vmem_mb=None):
    kw = dict(dimension_semantics=sem)
    if vmem_mb is not None:
        kw["vmem_limit_bytes"] = vmem_mb << 20
    return pltpu.CompilerParams(**kw)


def _pick(m, cands):
    for c in cands:
        if m % c == 0:
            return c
    raise ValueError(f"no tile for {m}")


def _dot(a, b):
    return jnp.dot(a, b, preferred_element_type=F32)


def _dot_nt(a, b):
    return lax.dot_general(a, b, (((1,), (1,)), ((), ())), preferred_element_type=F32)


def _silu(x):
    return x * jax.nn.sigmoid(x)


def _mod_kernel(c_ref, w_ref, b_ref, o_ref):
    s = _silu(c_ref[...])
    o_ref[...] = jnp.dot(s, w_ref[...], preferred_element_type=F32, precision=HI) + b_ref[...]


def _mod_params(cs, w_mod, b_mod, layer):
    g, d = cs.shape
    n = w_mod.shape[2]
    tn = _pick(n, (1024, 512))
    return pl.pallas_call(
        _mod_kernel,
        grid=(n // tn,),
        in_specs=[pl.BlockSpec((g, d), lambda j: (0, 0)),
                  pl.BlockSpec((None, d, tn), lambda j: (layer, 0, j)),
                  pl.BlockSpec((None, 1, tn), lambda j: (layer, 0, j))],
        out_specs=pl.BlockSpec((g, tn), lambda j: (0, j)),
        out_shape=jax.ShapeDtypeStruct((g, n), F32),
        compiler_params=_cparams(("arbitrary",), 40),
        name="mod_params",
    )(cs, w_mod, b_mod.reshape(b_mod.shape[0], 1, n))


def _layer_norm(x):
    mu = jnp.mean(x, -1, keepdims=True)
    xc = x - mu
    var = jnp.mean(xc * xc, -1, keepdims=True)
    return xc * lax.rsqrt(var + EPS)


def _ln_mod_kernel(x_ref, sh_ref, sc_ref, o_ref):
    y = _layer_norm(x_ref[...])
    o_ref[...] = (y * (1.0 + sc_ref[...]) + sh_ref[...]).astype(o_ref.dtype)


def _resid_ln_kernel(x_ref, m_ref, gt_ref, g_ref, b_ref, *rest, alpha, with_mod):
    y = _layer_norm(alpha * x_ref[...] + gt_ref[...] * m_ref[...])
    xn = y * g_ref[...] + b_ref[...]
    if with_mod:
        sh_ref, sc_ref, xo_ref, ho_ref = rest
        xo_ref[...] = xn
        ho_ref[...] = (_layer_norm(xn) * (1.0 + sc_ref[...]) + sh_ref[...]).astype(ho_ref.dtype)
    else:
        (xo_ref,) = rest
        xo_ref[...] = xn


def _group_of_tile(i, lat_tiles, tiles_per_seq, n_batch):
    return jnp.where(i < lat_tiles, i // tiles_per_seq, n_batch)


class _Rows:
    def __init__(self, n_batch, n_lat, n_ctx):
        self.b, self.l, self.c = n_batch, n_lat, n_ctx
        self.t = n_batch * n_lat
        self.r = n_batch * (n_lat + n_ctx)
        self.lat_tiles = self.t // ROW_TILE
        self.lat_tps = n_lat // ROW_TILE
        self.ctx_tps = n_ctx // ROW_TILE

    def group(self, i):
        return _group_of_tile(i, self.lat_tiles, self.lat_tps, self.b)


def _mod_spec(rows, piece, d):
    return pl.BlockSpec((None, None, 1, d), lambda i: (piece, rows.group(i), 0, 0))


def _ln_mod(x, mods, rows, m, shift_piece, scale_piece, out_dtype):
    d = x.shape[1]
    return pl.pallas_call(
        _ln_mod_kernel,
        grid=(m // ROW_TILE,),
        in_specs=[pl.BlockSpec((ROW_TILE, d), lambda i: (i, 0)),
                  _mod_spec(rows, shift_piece, d), _mod_spec(rows, scale_piece, d)],
        out_specs=pl.BlockSpec((ROW_TILE, d), lambda i: (i, 0)),
        out_shape=jax.ShapeDtypeStruct((m, d), out_dtype),
        compiler_params=_cparams(("parallel",)),
        name="ln_mod",
    )(x, mods, mods)


def _resid_ln(x, mres, mods, gate_piece, ln_g, ln_b, rows, m, alpha, next_mods=None, next_dtype=BF16):
    d = x.shape[1]
    row_spec = pl.BlockSpec((ROW_TILE, d), lambda i: (i, 0))
    vec_spec = pl.BlockSpec((1, d), lambda i: (0, 0))
    in_specs = [row_spec, row_spec, _mod_spec(rows, gate_piece, d), vec_spec, vec_spec]
    args = [x, mres, mods, ln_g, ln_b]
    if next_mods is None:
        out_specs = row_spec
        out_shape = jax.ShapeDtypeStruct((m, d), F32)
    else:
        nm, sh_piece, sc_piece = next_mods
        in_specs += [_mod_spec(rows, sh_piece, d), _mod_spec(rows, sc_piece, d)]
        args += [nm, nm]
        out_specs = (row_spec, row_spec)
        out_shape = (jax.ShapeDtypeStruct((m, d), F32), jax.ShapeDtypeStruct((m, d), next_dtype))
    return pl.pallas_call(
        functools.partial(_resid_ln_kernel, alpha=alpha, with_mod=next_mods is not None),
        grid=(m // ROW_TILE,),
        in_specs=in_specs, out_specs=out_specs, out_shape=out_shape,
        compiler_params=_cparams(("parallel",)),
        name="resid_ln",
    )(*args)


def _mm_kernel(x_ref, w_ref, o_ref):
    o_ref[...] = _dot(x_ref[...], w_ref[...]).astype(o_ref.dtype)


def _mm(x, w, m, out_dtype, tm, tn, n_outer=False, vmem_mb=48):
    k, n = w.shape
    if n_outer:
        grid = (n // tn, m // tm)
        xi, wi, oi = (lambda j, i: (i, 0)), (lambda j, i: (0, j)), (lambda j, i: (i, j))
    else:
        grid = (m // tm, n // tn)
        xi, wi, oi = (lambda i, j: (i, 0)), (lambda i, j: (0, j)), (lambda i, j: (i, j))
    return pl.pallas_call(
        _mm_kernel,
        grid=grid,
        in_specs=[pl.BlockSpec((tm, k), xi), pl.BlockSpec((k, tn), wi)],
        out_specs=pl.BlockSpec((tm, tn), oi),
        out_shape=jax.ShapeDtypeStruct((m, n), out_dtype),
        compiler_params=_cparams(("parallel", "parallel"), vmem_mb),
        name="mm",
    )(x, w)


def _swiglu_up_kernel(x_ref, w1_ref, w3_ref, o_ref):
    x = x_ref[...]
    o_ref[...] = (_silu(_dot(x, w1_ref[...])) * _dot(x, w3_ref[...])).astype(o_ref.dtype)


def _swiglu_up(x, w1, w3, m, tm, tf):
    k, f = w1.shape
    return pl.pallas_call(
        _swiglu_up_kernel,
        grid=(m // tm, f // tf),
        in_specs=[pl.BlockSpec((tm, k), lambda i, j: (i, 0)),
                  pl.BlockSpec((k, tf), lambda i, j: (0, j)),
                  pl.BlockSpec((k, tf), lambda i, j: (0, j))],
        out_specs=pl.BlockSpec((tm, tf), lambda i, j: (i, j)),
        out_shape=jax.ShapeDtypeStruct((m, f), BF16),
        compiler_params=_cparams(("parallel", "parallel"), 48),
        name="swiglu_up",
    )(x, w1, w3)


def _merge_kernel(ya_ref, yb_ref, wa_ref, wb_ref, ga_ref, gb_ref, o_ref):
    a = _dot(ya_ref[...], wa_ref[...])
    b = _dot(yb_ref[...], wb_ref[...])
    o_ref[...] = (jax.nn.sigmoid(ga_ref[...]) * a + jax.nn.sigmoid(gb_ref[...]) * b).astype(o_ref.dtype)


def _merge(ya, yb, wa, wb, gates, m, tm, tn):
    k, n = wa.shape
    nb = n // tn
    return pl.pallas_call(
        _merge_kernel,
        grid=(nb, m // tm),
        in_specs=[pl.BlockSpec((tm, k), lambda j, i: (i, 0)),
                  pl.BlockSpec((tm, k), lambda j, i: (i, 0)),
                  pl.BlockSpec((k, tn), lambda j, i: (0, j)),
                  pl.BlockSpec((k, tn), lambda j, i: (0, j)),
                  pl.BlockSpec((tm, tn), lambda j, i: (i, j)),
                  pl.BlockSpec((tm, tn), lambda j, i: (i, nb + j))],
        out_specs=pl.BlockSpec((tm, tn), lambda j, i: (i, j)),
        out_shape=jax.ShapeDtypeStruct((m, n), BF16),
        compiler_params=_cparams(("parallel", "parallel"), 48),
        name="merge",
    )(ya, yb, wa, wb, gates, gates)


CONV_COLS = 512
HALO = 8
INV_BLOCK = 16
assert CHUNK == 4 * INV_BLOCK


def _gdn_prep_kernel(x_ref, p_ref, n_ref, w_ref, o_ref, buf_ref, *, rows):
    i = pl.program_id(0)
    j = pl.program_id(1)
    is_lat = i < rows.lat_tiles
    pos = jnp.where(is_lat, i % rows.lat_tps, (i - rows.lat_tiles) % rows.ctx_tps)
    tps = jnp.where(is_lat, rows.lat_tps, rows.ctx_tps)
    buf_ref[0:HALO, :] = jnp.where(pos == 0, 0.0, p_ref[...])
    buf_ref[HALO:HALO + ROW_TILE, :] = x_ref[...]
    buf_ref[HALO + ROW_TILE:, :] = jnp.where(pos == tps - 1, 0.0, n_ref[...])
    base = HALO - CONV_K // 2
    acc = w_ref[0:1, :] * buf_ref[base:base + ROW_TILE, :]
    for t in range(1, CONV_K):
        acc = acc + w_ref[t:t + 1, :] * buf_ref[base + t:base + t + ROW_TILE, :]
    y = _silu(acc)
    kind = j // (QK_W // CONV_COLS)
    scale = jnp.where(kind == 0, HEAD_DIM ** -0.5, 1.0)
    for hh in range(CONV_COLS // HEAD_DIM):
        sl = slice(hh * HEAD_DIM, (hh + 1) * HEAD_DIM)
        seg = y[:, sl]
        ss = jnp.sum(seg * seg, -1, keepdims=True)
        nrm = seg * (lax.rsqrt(ss + EPS) * scale)
        o_ref[:, sl] = jnp.where(kind == 2, seg, nrm)


def _gdn_prep(qkv, conv_w, rows):
    r, n = qkv.shape
    hb = ROW_TILE // HALO
    last = r // HALO - 1
    return pl.pallas_call(
        functools.partial(_gdn_prep_kernel, rows=rows),
        grid=(r // ROW_TILE, n // CONV_COLS),
        in_specs=[pl.BlockSpec((ROW_TILE, CONV_COLS), lambda i, j: (i, j)),
                  pl.BlockSpec((HALO, CONV_COLS), lambda i, j: (jnp.maximum(i * hb - 1, 0), j)),
                  pl.BlockSpec((HALO, CONV_COLS), lambda i, j: (jnp.minimum((i + 1) * hb, last), j)),
                  pl.BlockSpec((CONV_K, CONV_COLS), lambda i, j: (0, j))],
        out_specs=pl.BlockSpec((ROW_TILE, CONV_COLS), lambda i, j: (i, j)),
        out_shape=jax.ShapeDtypeStruct((r, n), F32),
        scratch_shapes=[pltpu.VMEM((ROW_TILE + 2 * HALO, CONV_COLS), F32)],
        compiler_params=_cparams(("parallel", "parallel")),
        name="gdn_prep",
    )(qkv, qkv, qkv, conv_w)


def _softplus(x):
    return jnp.maximum(x, 0.0) + jnp.log(1.0 + jnp.exp(-jnp.abs(x)))


def _gdn_scan_kernel(q_ref, k_ref, v_ref, ab_ref, abt_ref, al_ref, dtb_ref, alt_ref, dtbt_ref,
                     o_ref, st_ref):
    d = pl.program_id(0)
    s = pl.program_id(2)
    fwd = d == 0

    @pl.when(s == 0)
    def _():
        st_ref[...] = jnp.zeros_like(st_ref)

    nh = HEADS
    ab = ab_ref[...]
    g_all = -jnp.exp(al_ref[...]) * _softplus(ab[:, 0:2 * nh] + dtb_ref[...])
    beta_all = jax.nn.sigmoid(ab[:, 2 * nh:4 * nh])
    g_d = jnp.where(fwd, g_all[:, 0:nh], g_all[:, nh:2 * nh])
    beta_d = jnp.where(fwd, beta_all[:, 0:nh], beta_all[:, nh:2 * nh])
    abt = abt_ref[...]
    gt_all = -jnp.exp(alt_ref[...]) * _softplus(abt[0:2 * nh, :] + dtbt_ref[...])
    gt_d = jnp.where(fwd, gt_all[0:nh, :], gt_all[nh:2 * nh, :])

    ri = lax.broadcasted_iota(jnp.int32, (CHUNK, CHUNK), 0)
    ci = lax.broadcasted_iota(jnp.int32, (CHUNK, CHUNK), 1)
    order = (ri - ci) * jnp.where(fwd, 1, -1)
    incl = order >= 0
    strict = order > 0
    incl_t = order <= 0
    eye = (ri == ci).astype(F32)
    shift = INV_BLOCK.bit_length() - 1
    blk_xor = lax.shift_right_logical(ri, shift) ^ lax.shift_right_logical(ci, shift)
    blk_a = blk_xor == 0
    off_b = blk_xor == 1
    off_c = blk_xor >= 2
    gc_col = jnp.dot(incl.astype(F32), g_d, preferred_element_type=F32, precision=HI)
    gc_row = jnp.dot(gt_d, incl_t.astype(F32), preferred_element_type=F32, precision=HI)
    gl_row = jnp.sum(g_d, axis=0, keepdims=True)

    for h in range(nh):
        sl = slice(h * HEAD_DIM, (h + 1) * HEAD_DIM)
        qh, kh, vh = q_ref[:, sl], k_ref[:, sl], v_ref[:, sl]
        gcc = gc_col[:, h:h + 1]
        gcr = gc_row[h:h + 1, :]
        bc = beta_d[:, h:h + 1]
        gl = gl_row[:, h:h + 1]
        decay = jnp.where(incl, jnp.exp(gcc - gcr), 0.0)
        kb = kh * bc
        kh16 = kh.astype(BF16)
        kkqk = _dot_nt(jnp.concatenate([kb, qh], axis=0).astype(BF16), kh16)
        lmat = jnp.where(strict, kkqk[0:CHUNK] * decay, 0.0)
        a_intra = kkqk[CHUNK:] * decay
        p = -jnp.where(blk_a, lmat, 0.0)
        inv = eye + p
        for _ in range(INV_BLOCK.bit_length() - 2):
            p16 = p.astype(BF16)
            p = _dot(p16, p16)
            inv = inv + _dot(inv.astype(BF16), p.astype(BF16))
        for off in (off_b, off_c):
            inv16 = inv.astype(BF16)
            inv = inv - _dot(inv16, _dot(jnp.where(off, lmat, 0.0).astype(BF16), inv16).astype(BF16))
        eg = jnp.exp(gcc)
        uw = _dot(inv.astype(BF16), jnp.concatenate([vh * bc, kb * eg], axis=1).astype(BF16))
        u, w = uw[:, 0:HEAD_DIM], uw[:, HEAD_DIM:]
        q_dec = qh * eg
        k_dec = kh * jnp.exp(gl - gcc)
        state = st_ref[h]
        wq = _dot(jnp.concatenate([w, q_dec], axis=0).astype(BF16), state.astype(BF16))
        v_new = u - wq[0:CHUNK]
        v16 = v_new.astype(BF16)
        o_ref[:, sl] = wq[CHUNK:] + _dot(a_intra.astype(BF16), v16)
        st_ref[h] = state * jnp.exp(gl) + _dot(k_dec.T.astype(BF16), v16)


def _gdn_scan(qkvn, small, abt3, a_log, dt_bias, rows, ab_block):
    r = qkvn.shape[0]
    ncc, ncl = rows.c // CHUNK, rows.l // CHUNK
    lat_blocks = rows.t // CHUNK

    def rb(d, b, s):
        cc = jnp.where(d == 0, s, ncc - 1 - s)
        lc = jnp.where(d == 0, s - ncc, ncc + ncl - 1 - s)
        return jnp.where(s < ncc, lat_blocks + b * ncc + cc, b * ncl + lc)

    w = QK_W
    al = a_log.reshape(1, 2 * HEADS)
    dtb = dt_bias.reshape(1, 2 * HEADS)
    return pl.pallas_call(
        _gdn_scan_kernel,
        grid=(2, rows.b, ncc + ncl),
        in_specs=[pl.BlockSpec((CHUNK, w), lambda d, b, s: (rb(d, b, s), 0)),
                  pl.BlockSpec((CHUNK, w), lambda d, b, s: (rb(d, b, s), 1)),
                  pl.BlockSpec((CHUNK, w), lambda d, b, s: (rb(d, b, s), 2)),
                  pl.BlockSpec((CHUNK, LANES), lambda d, b, s: (rb(d, b, s), ab_block)),
                  pl.BlockSpec((None, CHUNK, CHUNK), lambda d, b, s: (rb(d, b, s), 0, 0)),
                  pl.BlockSpec((1, 2 * HEADS), lambda d, b, s: (0, 0)),
                  pl.BlockSpec((1, 2 * HEADS), lambda d, b, s: (0, 0)),
                  pl.BlockSpec((2 * HEADS, 1), lambda d, b, s: (0, 0)),
                  pl.BlockSpec((2 * HEADS, 1), lambda d, b, s: (0, 0))],
        out_specs=pl.BlockSpec((None, CHUNK, w), lambda d, b, s: (d, rb(d, b, s), 0)),
        out_shape=jax.ShapeDtypeStruct((2, r, w), F32),
        scratch_shapes=[pltpu.VMEM((HEADS, HEAD_DIM, HEAD_DIM), F32)],
        compiler_params=_cparams(("parallel", "parallel", "arbitrary")),
        name="gdn_scan",
    )(qkvn, qkvn, qkvn, small, abt3, al, dtb, al.reshape(-1, 1), dtb.reshape(-1, 1))


def _gdn_out_kernel(o_ref, z_ref, g_ref, y_ref):
    o = o_ref[0] + o_ref[1]
    z = z_ref[...]
    for h in range(HEADS):
        sl = slice(h * HEAD_DIM, (h + 1) * HEAD_DIM)
        seg = o[:, sl]
        ms = jnp.mean(seg * seg, -1, keepdims=True)
        y_ref[:, sl] = (seg * lax.rsqrt(ms + EPS) * g_ref[...] * _silu(z[:, sl])).astype(y_ref.dtype)


def _gdn_out(o2, z, gdn_norm, m):
    w = QK_W
    return pl.pallas_call(
        _gdn_out_kernel,
        grid=(m // ROW_TILE,),
        in_specs=[pl.BlockSpec((2, ROW_TILE, w), lambda i: (0, i, 0)),
                  pl.BlockSpec((ROW_TILE, w), lambda i: (i, 0)),
                  pl.BlockSpec((1, HEAD_DIM), lambda i: (0, 0))],
        out_specs=pl.BlockSpec((ROW_TILE, w), lambda i: (i, 0)),
        out_shape=jax.ShapeDtypeStruct((m, w), BF16),
        compiler_params=_cparams(("parallel",)),
        name="gdn_out",
    )(o2, z, gdn_norm.reshape(1, HEAD_DIM))


Q_HEAD_W = 2 * HEAD_DIM
UQ_HEAD_COLS = 3 * HEAD_DIM


def _rms(x, g):
    return x * lax.rsqrt(jnp.mean(x * x, -1, keepdims=True) + EPS) * g


def _mla_q_kernel(x_ref, g_ref, w_ref, tab_ref, o_ref):
    xn = _rms(x_ref[...], g_ref[...]).astype(BF16)
    cos_s = tab_ref[:, 0:LANES]
    sin_s = tab_ref[:, LANES:2 * LANES]
    for h in range(HEADS):
        r = _dot(xn, w_ref[:, h * UQ_HEAD_COLS:(h + 1) * UQ_HEAD_COLS])
        o_ref[:, h * Q_HEAD_W:h * Q_HEAD_W + LANES] = (r[:, 0:LANES] * MLA_SCALE).astype(o_ref.dtype)
        rope = r[:, LANES:2 * LANES] * cos_s + r[:, 2 * LANES:] * sin_s
        o_ref[:, h * Q_HEAD_W + LANES:(h + 1) * Q_HEAD_W] = rope.astype(o_ref.dtype)


def _mla_q(small, q_norm, w_ext, tab, m, tm):
    return pl.pallas_call(
        _mla_q_kernel,
        grid=(m // tm,),
        in_specs=[pl.BlockSpec((tm, Q_LORA), lambda i: (i, 0)),
                  pl.BlockSpec((1, Q_LORA), lambda i: (0, 0)),
                  pl.BlockSpec(w_ext.shape, lambda i: (0, 0)),
                  pl.BlockSpec((tm, 4 * LANES), lambda i: (i, 0))],
        out_specs=pl.BlockSpec((tm, HEADS * Q_HEAD_W), lambda i: (i, 0)),
        out_shape=jax.ShapeDtypeStruct((m, HEADS * Q_HEAD_W), BF16),
        compiler_params=_cparams(("parallel",), 48),
        name="mla_q",
    )(small, q_norm.reshape(1, Q_LORA), w_ext, tab)


def _mla_kv_kernel(x_ref, kr_ref, g_ref, w_ref, tab_ref, k_ref, v_ref):
    xn = _rms(x_ref[...], g_ref[...]).astype(BF16)
    kr = kr_ref[...]
    rope = kr[:, 0:LANES] * tab_ref[:, 2 * LANES:3 * LANES] + kr[:, LANES:] * tab_ref[:, 3 * LANES:]
    rope = rope.astype(k_ref.dtype)
    for h in range(HEADS):
        r = _dot(xn, w_ref[:, h * Q_HEAD_W:(h + 1) * Q_HEAD_W])
        k_ref[:, h * Q_HEAD_W:h * Q_HEAD_W + LANES] = r[:, 0:LANES].astype(k_ref.dtype)
        k_ref[:, h * Q_HEAD_W + LANES:(h + 1) * Q_HEAD_W] = rope
        v_ref[:, h * HEAD_DIM:(h + 1) * HEAD_DIM] = r[:, LANES:].astype(v_ref.dtype)


def _mla_kv(small, kv_norm, w_ukv, tab, m, tm):
    return pl.pallas_call(
        _mla_kv_kernel,
        grid=(m // tm,),
        in_specs=[pl.BlockSpec((tm, KV_LORA), lambda i: (i, 1)),
                  pl.BlockSpec((tm, 2 * LANES), lambda i: (i, 4)),
                  pl.BlockSpec((1, KV_LORA), lambda i: (0, 0)),
                  pl.BlockSpec(w_ukv.shape, lambda i: (0, 0)),
                  pl.BlockSpec((tm, 4 * LANES), lambda i: (i, 0))],
        out_specs=(pl.BlockSpec((tm, HEADS * Q_HEAD_W), lambda i: (i, 0)),
                   pl.BlockSpec((tm, HEADS * HEAD_DIM), lambda i: (i, 0))),
        out_shape=(jax.ShapeDtypeStruct((m, HEADS * Q_HEAD_W), BF16),
                   jax.ShapeDtypeStruct((m, HEADS * HEAD_DIM), BF16)),
        compiler_params=_cparams(("parallel",), 48),
        name="mla_kv",
    )(small, small, kv_norm.reshape(1, KV_LORA), w_ukv, tab)


def _attn_kernel(*refs, with_latent):
    if with_latent:
        q_ref, kl_ref, vl_ref, kc_ref, vc_ref, o_ref = refs
    else:
        q_ref, kc_ref, vc_ref, o_ref = refs
    q = q_ref[...]
    sc = _dot_nt(q, kc_ref[...])
    m = jnp.max(sc, -1, keepdims=True)
    if with_latent:
        sl = _dot_nt(q, kl_ref[...])
        m = jnp.maximum(m, jnp.max(sl, -1, keepdims=True))
    pc = jnp.exp(sc - m)
    den = jnp.sum(pc, -1, keepdims=True)
    acc = _dot(pc.astype(BF16), vc_ref[...])
    if with_latent:
        p_lat = jnp.exp(sl - m)
        den = den + jnp.sum(p_lat, -1, keepdims=True)
        acc = acc + _dot(p_lat.astype(BF16), vl_ref[...])
    o_ref[...] = (acc / den).astype(o_ref.dtype)


def _attention(qf, kf, vf, rows, latent_queries, tq):
    b, l, c = rows.b, rows.l, rows.c
    ctx0 = rows.t // c
    kc_spec = pl.BlockSpec((c, Q_HEAD_W), lambda bi, h, qi: (ctx0 + bi, h))
    vc_spec = pl.BlockSpec((c, HEAD_DIM), lambda bi, h, qi: (ctx0 + bi, h))
    if latent_queries:
        nq = l // tq
        q_spec = pl.BlockSpec((tq, Q_HEAD_W), lambda bi, h, qi: (bi * nq + qi, h))
        o_spec = pl.BlockSpec((tq, HEAD_DIM), lambda bi, h, qi: (bi * nq + qi, h))
        in_specs = [q_spec,
                    pl.BlockSpec((l, Q_HEAD_W), lambda bi, h, qi: (bi, h)),
                    pl.BlockSpec((l, HEAD_DIM), lambda bi, h, qi: (bi, h)),
                    kc_spec, vc_spec]
        args = (qf, kf, vf, kf, vf)
        m_out = rows.t
    else:
        nq = c // tq
        q0 = rows.t // tq
        q_spec = pl.BlockSpec((tq, Q_HEAD_W), lambda bi, h, qi: (q0 + bi * nq + qi, h))
        o_spec = pl.BlockSpec((tq, HEAD_DIM), lambda bi, h, qi: (bi * nq + qi, h))
        in_specs = [q_spec, kc_spec, vc_spec]
        args = (qf, kf, vf)
        m_out = rows.r - rows.t
    return pl.pallas_call(
        functools.partial(_attn_kernel, with_latent=latent_queries),
        grid=(b, HEADS, nq),
        in_specs=in_specs, out_specs=o_spec,
        out_shape=jax.ShapeDtypeStruct((m_out, HEADS * HEAD_DIM), BF16),
        compiler_params=_cparams(("parallel", "parallel", "arbitrary"), 48),
        name="mla_attn",
    )(*args)


def _router_kernel(h_ref, w_ref, ri_ref, pos_ref, cnt_ref, carry_ref, *, n_experts):
    i = pl.program_id(0)

    @pl.when(i == 0)
    def _():
        carry_ref[...] = jnp.zeros_like(carry_ref)

    logits = jnp.dot(h_ref[...], w_ref[...], preferred_element_type=F32, precision=HI)
    lane = lax.broadcasted_iota(jnp.int32, logits.shape, 1).astype(F32)
    neg = -jnp.inf
    logits = jnp.where(lane < n_experts, logits, neg)
    m1 = jnp.max(logits, -1, keepdims=True)
    i1 = jnp.min(jnp.where(logits == m1, lane, float(LANES)), -1, keepdims=True)
    sel1 = lane == i1
    rest = jnp.where(sel1, neg, logits)
    m2 = jnp.max(rest, -1, keepdims=True)
    i2 = jnp.min(jnp.where(rest == m2, lane, float(LANES)), -1, keepdims=True)
    sel2 = lane == i2
    e = jnp.exp(m2 - m1)
    w1 = 1.0 / (1.0 + e)
    w2 = e / (1.0 + e)
    ri_ref[...] = jnp.where(lane == 0, i1,
                            jnp.where(lane == 1, i2,
                                      jnp.where(lane == 2, w1, jnp.where(lane == 3, w2, 0.0))))
    onehot = jnp.where(sel1, 1.0, jnp.where(sel2, 1.0, 0.0))
    tile = logits.shape[0]
    ri_t = lax.broadcasted_iota(jnp.int32, (tile, tile), 0)
    ci_t = lax.broadcasted_iota(jnp.int32, (tile, tile), 1)
    before = (ci_t < ri_t).astype(BF16)
    pos_ref[...] = _dot(before, onehot.astype(BF16)) + carry_ref[0:1, :]
    carry_ref[0:1, :] = carry_ref[0:1, :] + jnp.sum(onehot, axis=0, keepdims=True)
    cnt_ref[...] = jnp.broadcast_to(carry_ref[0:1, :], cnt_ref.shape)


def _router(hf, router_w, n_experts):
    t, d = hf.shape
    wpad = jnp.zeros((d, LANES), F32).at[:, :n_experts].set(router_w)
    return pl.pallas_call(
        functools.partial(_router_kernel, n_experts=n_experts),
        grid=(t // ROW_TILE,),
        in_specs=[pl.BlockSpec((ROW_TILE, d), lambda i: (i, 0)),
                  pl.BlockSpec((d, LANES), lambda i: (0, 0))],
        out_specs=(pl.BlockSpec((ROW_TILE, LANES), lambda i: (i, 0)),
                   pl.BlockSpec((ROW_TILE, LANES), lambda i: (i, 0)),
                   pl.BlockSpec((8, LANES), lambda i: (0, 0))),
        out_shape=(jax.ShapeDtypeStruct((t, LANES), F32),
                   jax.ShapeDtypeStruct((t, LANES), F32),
                   jax.ShapeDtypeStruct((8, LANES), F32)),
        scratch_shapes=[pltpu.VMEM((8, LANES), F32)],
        compiler_params=_cparams(("arbitrary",)),
        name="moe_router",
    )(hf, wpad)


def _row_copy(src_hbm, row, buf, r, sem):
    return pltpu.make_async_copy(src_hbm.at[pl.ds(row, 1), :], buf.at[pl.ds(r, 1), :], sem)


def _gather_rows(idx_ref, base, n, src_hbm, buf, sem):
    def issue(r, c):
        _row_copy(src_hbm, idx_ref[base + r], buf, r, sem).start()
        return c

    def wait(r, c):
        _row_copy(src_hbm, 0, buf, r, sem).wait()
        return c

    lax.fori_loop(0, n, issue, 0)
    lax.fori_loop(0, n, wait, 0)


def _moe_gather_kernel(valid_ref, tok_ref, h_hbm, o_ref, buf, sem):
    i = pl.program_id(0)

    @pl.when(valid_ref[i] > 0)
    def _():
        _gather_rows(tok_ref, i * MOE_TILE, MOE_TILE, h_hbm, buf, sem)
        o_ref[...] = buf[...].astype(o_ref.dtype)

    @pl.when(valid_ref[i] == 0)
    def _():
        o_ref[...] = jnp.zeros_like(o_ref)


def _moe_gather(valid, tok, hf, n_slots):
    d = hf.shape[1]
    return pl.pallas_call(
        _moe_gather_kernel,
        grid_spec=pltpu.PrefetchScalarGridSpec(
            num_scalar_prefetch=2,
            grid=(n_slots // MOE_TILE,),
            in_specs=[pl.BlockSpec(memory_space=pl.ANY)],
            out_specs=pl.BlockSpec((MOE_TILE, d), lambda i, v, t: (i, 0)),
            scratch_shapes=[pltpu.VMEM((MOE_TILE, d), F32), pltpu.SemaphoreType.DMA]),
        out_shape=jax.ShapeDtypeStruct((n_slots, d), BF16),
        compiler_params=_cparams(("arbitrary",)),
        name="moe_gather",
    )(valid, tok, hf)


def _moe_up_kernel(te_ref, valid_ref, x_ref, w1_ref, w3_ref, o_ref):
    i = pl.program_id(1)

    @pl.when(valid_ref[i] > 0)
    def _():
        x = x_ref[...]
        o_ref[...] = (_silu(_dot(x, w1_ref[...])) * _dot(x, w3_ref[...])).astype(o_ref.dtype)

    @pl.when(valid_ref[i] == 0)
    def _():
        o_ref[...] = jnp.zeros_like(o_ref)


def _moe_up(te, valid, xs, w1, w3, tf):
    s, d = xs.shape
    f = w1.shape[2]
    return pl.pallas_call(
        _moe_up_kernel,
        grid_spec=pltpu.PrefetchScalarGridSpec(
            num_scalar_prefetch=2,
            grid=(f // tf, s // MOE_TILE),
            in_specs=[pl.BlockSpec((MOE_TILE, d), lambda j, i, te, v: (i, 0)),
                      pl.BlockSpec((None, d, tf), lambda j, i, te, v: (te[i], 0, j)),
                      pl.BlockSpec((None, d, tf), lambda j, i, te, v: (te[i], 0, j))],
            out_specs=pl.BlockSpec((MOE_TILE, tf), lambda j, i, te, v: (i, j))),
        out_shape=jax.ShapeDtypeStruct((s, f), BF16),
        compiler_params=_cparams(("arbitrary", "arbitrary"), 48),
        name="moe_up",
    )(te, valid, xs, w1, w3)


def _moe_down_kernel(te_ref, valid_ref, x_ref, w_ref, o_ref):
    i = pl.program_id(1)

    @pl.when(valid_ref[i] > 0)
    def _():
        o_ref[...] = _dot(x_ref[...], w_ref[...])

    @pl.when(valid_ref[i] == 0)
    def _():
        o_ref[...] = jnp.zeros_like(o_ref)


def _moe_down(te, valid, hmid, w2, tn):
    s, f = hmid.shape
    d = w2.shape[2]
    return pl.pallas_call(
        _moe_down_kernel,
        grid_spec=pltpu.PrefetchScalarGridSpec(
            num_scalar_prefetch=2,
            grid=(d // tn, s // MOE_TILE),
            in_specs=[pl.BlockSpec((MOE_TILE, f), lambda j, i, te, v: (i, 0)),
                      pl.BlockSpec((None, f, tn), lambda j, i, te, v: (te[i], 0, j))],
            out_specs=pl.BlockSpec((MOE_TILE, tn), lambda j, i, te, v: (i, j))),
        out_shape=jax.ShapeDtypeStruct((s, d), F32),
        compiler_params=_cparams(("arbitrary", "arbitrary"), 48),
        name="moe_down",
    )(te, valid, hmid, w2)


def _moe_combine_kernel(s1_ref, s2_ref, y_hbm, ri_ref, x_ref, gt_ref, g_ref, b_ref, o_ref,
                        buf1, buf2, sem1, sem2, *, alpha):
    i = pl.program_id(0)
    base = i * ROW_TILE

    def issue(r, c):
        _row_copy(y_hbm, s1_ref[base + r], buf1, r, sem1).start()
        _row_copy(y_hbm, s2_ref[base + r], buf2, r, sem2).start()
        return c

    def wait(r, c):
        _row_copy(y_hbm, 0, buf1, r, sem1).wait()
        _row_copy(y_hbm, 0, buf2, r, sem2).wait()
        return c

    lax.fori_loop(0, ROW_TILE, issue, 0)
    lax.fori_loop(0, ROW_TILE, wait, 0)
    ri = ri_ref[...]
    y = ri[:, 2:3] * buf1[...] + ri[:, 3:4] * buf2[...]
    xn = _layer_norm(alpha * x_ref[...] + gt_ref[...] * y)
    o_ref[...] = xn * g_ref[...] + b_ref[...]


def _moe_combine(slot1, slot2, yslot, ri, x, mods, gate_piece, ln_g, ln_b, rows, alpha):
    t, d = x.shape
    row_spec = pl.BlockSpec((ROW_TILE, d), lambda i, a, b: (i, 0))
    vec_spec = pl.BlockSpec((1, d), lambda i, a, b: (0, 0))
    return pl.pallas_call(
        functools.partial(_moe_combine_kernel, alpha=alpha),
        grid_spec=pltpu.PrefetchScalarGridSpec(
            num_scalar_prefetch=2,
            grid=(t // ROW_TILE,),
            in_specs=[pl.BlockSpec(memory_space=pl.ANY),
                      pl.BlockSpec((ROW_TILE, LANES), lambda i, a, b: (i, 0)),
                      row_spec,
                      pl.BlockSpec((None, None, 1, d), lambda i, a, b: (gate_piece, rows.group(i), 0, 0)),
                      vec_spec, vec_spec],
            out_specs=row_spec,
            scratch_shapes=[pltpu.VMEM((ROW_TILE, d), F32), pltpu.VMEM((ROW_TILE, d), F32),
                            pltpu.SemaphoreType.DMA, pltpu.SemaphoreType.DMA]),
        out_shape=jax.ShapeDtypeStruct((t, d), F32),
        compiler_params=_cparams(("arbitrary",)),
        name="moe_combine",
    )(slot1, slot2, yslot, ri, x, mods, ln_g, ln_b)


def _moe_plan(ri, pos, cnt, n_experts, n_slots):
    t = ri.shape[0]
    i1 = ri[:, 0].astype(jnp.int32)
    i2 = ri[:, 1].astype(jnp.int32)
    pos8 = pos[:, :n_experts].astype(jnp.int32)
    counts = cnt[0, :n_experts].astype(jnp.int32)
    padded = ((counts + MOE_TILE - 1) // MOE_TILE) * MOE_TILE
    ends = jnp.cumsum(padded)
    offs = ends - padded
    slot1 = offs[i1] + jnp.take_along_axis(pos8, i1[:, None], axis=1)[:, 0]
    slot2 = offs[i2] + jnp.take_along_axis(pos8, i2[:, None], axis=1)[:, 0]
    ids = jnp.arange(t, dtype=jnp.int32)
    tok = jnp.zeros((n_slots,), jnp.int32).at[slot1].set(ids).at[slot2].set(ids)
    starts = jnp.arange(n_slots // MOE_TILE, dtype=jnp.int32) * MOE_TILE
    valid = (starts < ends[-1]).astype(jnp.int32)
    te = jnp.minimum(jnp.searchsorted(ends, starts, side="right").astype(jnp.int32), n_experts - 1)
    te = jnp.where(valid > 0, te, jnp.max(jnp.where(valid > 0, te, 0)))
    return slot1, slot2, tok, te, valid


def _rope_tables(rows):
    l = rows.l
    nrow = l // GRID_W
    row = jnp.repeat(jnp.arange(nrow), GRID_W).astype(F32)
    col = jnp.tile(jnp.arange(GRID_W), nrow).astype(F32)
    n_freq = ROPE_DIM // 4
    inv_freq = ROPE_THETA ** (-jnp.arange(n_freq, dtype=F32) / n_freq)
    cr, sr = jnp.cos(row[:, None] * inv_freq), jnp.sin(row[:, None] * inv_freq)
    cc, sc = jnp.cos(col[:, None] * inv_freq), jnp.sin(col[:, None] * inv_freq)
    cos = jnp.concatenate([cr, cr, cc, cc], -1)
    sin = jnp.concatenate([-sr, sr, -sc, sc], -1)
    nctx = rows.r - rows.t
    cos = jnp.concatenate([jnp.tile(cos, (rows.b, 1)), jnp.ones((nctx, ROPE_DIM), F32)], 0)
    sin = jnp.concatenate([jnp.tile(sin, (rows.b, 1)), jnp.zeros((nctx, ROPE_DIM), F32)], 0)
    z = jnp.zeros((rows.r, LANES - ROPE_DIM), F32)
    return jnp.concatenate([MLA_SCALE * cos, z, MLA_SCALE * sin, z, cos, z, sin, z], -1)


_ROPE_SWAP = np.concatenate([np.arange(16, 32), np.arange(0, 16), np.arange(48, 64), np.arange(32, 48)])


def _uq_ext(w_uq):
    k = w_uq.shape[0]
    w = w_uq.reshape(k, HEADS, HEAD_DIM + ROPE_DIM)
    nope, rope = w[:, :, :HEAD_DIM], w[:, :, HEAD_DIM:]
    z = jnp.zeros((k, HEADS, LANES - ROPE_DIM), w.dtype)
    ext = jnp.concatenate([nope, rope, z, rope[:, :, _ROPE_SWAP], z], -1)
    return ext.reshape(k, HEADS * UQ_HEAD_COLS).astype(BF16)


def kernel(x, c, ctx, c_ctx, w_mod, b_mod, w_in, conv_w, a_log, dt_bias, gdn_norm, q_norm, kv_norm, w_uq, w_ukv, w_br_a, w_br_b, w_out, ln1_g, ln1_b, ln2_g, ln2_b, ffn_w1, ffn_w3, ffn_w2, moe_router, moe_w1, moe_w3, moe_w2):
    bsz, n_lat, d = x.shape
    n_ctx = ctx.shape[1]
    depth = w_mod.shape[0]
    assert depth == 2 and ffn_w1.shape[0] == 1 and moe_w1.shape[0] == 1
    rows = _Rows(bsz, n_lat, n_ctx)
    r_all, t_lat = rows.r, rows.t
    alpha = (2 * depth) ** 0.25
    n_experts = moe_router.shape[2]

    xs = jnp.concatenate([x.reshape(t_lat, d), ctx.reshape(r_all - t_lat, d)], 0)
    n_groups = -(-(bsz + 1) // 8) * 8
    cs = jnp.zeros((n_groups, d), F32).at[:bsz].set(c).at[bsz].set(c_ctx)
    tab = _rope_tables(rows)

    o_z = 3 * QK_W
    o_a = o_z + QK_W
    o_b = o_a + 2 * HEADS
    o_dq = o_b + 2 * HEADS
    o_dkv = o_dq + Q_LORA
    o_kr = o_dkv + KV_LORA
    o_ga = o_kr + ROPE_DIM
    ab_block = (Q_LORA + KV_LORA + 2 * LANES) // LANES

    tm_all = _pick(r_all, (1024, 512, 256))
    tm_lat = _pick(t_lat, (1024, 512, 256))
    tm_half = _pick(r_all, (512, 256))

    all_mods = [_mod_params(cs, w_mod, b_mod, i).reshape(n_groups, 6, d).transpose(1, 0, 2)[:, :, None, :]
                for i in range(depth)]

    h = None
    for i in range(depth):
        last = i == depth - 1
        m_mix = t_lat if last else r_all
        tm_mix = tm_lat if last else tm_all
        mods = all_mods[i]
        if i == 0:
            h = _ln_mod(xs, mods, rows, r_all, 0, 1, BF16)

        wi = w_in[i]
        kr_w = wi[:, o_kr:o_ga]
        zc = jnp.zeros((d, LANES - ROPE_DIM), F32)
        w_small = jnp.concatenate(
            [wi[:, o_dq:o_kr], kr_w, zc, kr_w[:, _ROPE_SWAP], zc, wi[:, o_a:o_dq], zc], 1).astype(BF16)
        qkv = _mm(h, wi[:, :o_z].astype(BF16), r_all, F32, tm_all, _pick(o_z, (1536, 1024, 512)))
        small = _mm(h, w_small, r_all, F32, tm_all, w_small.shape[1])
        z = _mm(h, wi[:, o_z:o_a].astype(BF16), m_mix, F32, tm_mix, 1024)
        gates = _mm(h, wi[:, o_ga:].astype(BF16), m_mix, F32, tm_mix, 1024)

        qkvn = _gdn_prep(qkv, conv_w[i], rows)
        ab = small[:, ab_block * LANES:ab_block * LANES + 4 * HEADS]
        abt3 = ab.reshape(r_all // CHUNK, CHUNK, 4 * HEADS).transpose(0, 2, 1)
        o2 = _gdn_scan(qkvn, small, abt3, a_log[i], dt_bias[i], rows, ab_block)
        ya = _gdn_out(o2, z, gdn_norm[i], m_mix)

        qf = _mla_q(small, q_norm[i], _uq_ext(w_uq[i]), tab, m_mix, _pick(m_mix, (512, 256)))
        kf, vf = _mla_kv(small, kv_norm[i], w_ukv[i].astype(BF16), tab, r_all, tm_half)
        yb = _attention(qf, kf, vf, rows, True, _pick(n_lat, (256,)))
        if not last:
            yb_ctx = _attention(qf, kf, vf, rows, False, _pick(n_ctx, (256,)))
            yb = jnp.concatenate([yb, yb_ctx], 0)

        ym = _merge(ya, yb, w_br_a[i].astype(BF16), w_br_b[i].astype(BF16), gates, m_mix,
                    _pick(m_mix, (512, 256)), 1024)
        mres = _mm(ym, w_out[i].astype(BF16), m_mix, F32, tm_mix, 1024)
        moe_layer = i % 2 == 1
        x1, h2 = _resid_ln(xs, mres, mods, 2, ln1_g[i].reshape(1, d), ln1_b[i].reshape(1, d), rows, m_mix,
                           alpha, next_mods=(mods, 3, 4), next_dtype=F32 if moe_layer else BF16)

        g2, b2 = ln2_g[i].reshape(1, d), ln2_b[i].reshape(1, d)
        if not moe_layer:
            j = i // 2
            hmid = _swiglu_up(h2, ffn_w1[j].astype(BF16), ffn_w3[j].astype(BF16), m_mix, tm_mix, 512)
            f_out = _mm(hmid, ffn_w2[j].astype(BF16), m_mix, F32, _pick(m_mix, (512, 256)), 1024, n_outer=True)
            if last:
                xs = _resid_ln(x1, f_out, mods, 5, g2, b2, rows, m_mix, alpha)
            else:
                xs, h = _resid_ln(x1, f_out, mods, 5, g2, b2, rows, m_mix, alpha,
                                  next_mods=(all_mods[i + 1], 0, 1))
        else:
            assert last
            j = i // 2
            ri, pos, cnt = _router(h2, moe_router[j], n_experts)
            n_slots = TOP_K * t_lat + n_experts * MOE_TILE
            slot1, slot2, tok, te, valid = _moe_plan(ri, pos, cnt, n_experts, n_slots)
            xg = _moe_gather(valid, tok, h2, n_slots)
            hmid = _moe_up(te, valid, xg, moe_w1[j].astype(BF16), moe_w3[j].astype(BF16), 512)
            yslot = _moe_down(te, valid, hmid, moe_w2[j].astype(BF16), 512)
            xs = _moe_combine(slot1, slot2, yslot, ri, x1, mods, 5, g2, b2, rows, alpha)
    return xs[:t_lat].reshape(bsz, n_lat, d)
```

```python
import functools
import math

import numpy as np
import jax
import jax.numpy as jnp
from jax import lax
from jax.experimental import pallas as pl
from jax.experimental.pallas import tpu as pltpu

F32 = jnp.float32
BF16 = jnp.bfloat16
HI = lax.Precision.HIGHEST

HEADS = 16
HEAD_DIM = 128
ROPE_DIM = 64
Q_LORA = 512
KV_LORA = 512
CHUNK = 64
CONV_K = 5
GRID_W = 64
ROPE_THETA = 10000.0
QK_W = HEADS * HEAD_DIM
MLA_SCALE = (HEAD_DIM + ROPE_DIM) ** -0.5
Q_SCALE = MLA_SCALE * math.log2(math.e)
TOP_K = 2
EPS = 1e-6
ROW_TILE = 256
MOE_TILE = 512
LANES = 128


def _cparams(sem, vmem_mb=None):
    kw = dict(dimension_semantics=sem)
    if vmem_mb is not None:
        kw["vmem_limit_bytes"] = vmem_mb << 20
    return pltpu.CompilerParams(**kw)


def _pick(m, cands):
    for c in cands:
        if m % c == 0:
            return c
    raise ValueError(f"no tile for {m}")


def _dot(a, b):
    return jnp.dot(a, b, preferred_element_type=F32)


def _dot_nt(a, b):
    return lax.dot_general(a, b, (((1,), (1,)), ((), ())), preferred_element_type=F32)


def _silu(x):
    return x * jax.nn.sigmoid(x)


def _mod_kernel(c_ref, w_ref, b_ref, o_ref):
    s = _silu(c_ref[...])
    o_ref[...] = jnp.dot(s, w_ref[...], preferred_element_type=F32, precision=HI) + b_ref[...]


def _mod_params(cs, w_mod, b_mod, layer):
    g, d = cs.shape
    n = w_mod.shape[2]
    tn = _pick(n, (1024, 512))
    return pl.pallas_call(
        _mod_kernel,
        grid=(n // tn,),
        in_specs=[pl.BlockSpec((g, d), lambda j: (0, 0)),
                  pl.BlockSpec((None, d, tn), lambda j: (layer, 0, j)),
                  pl.BlockSpec((None, 1, tn), lambda j: (layer, 0, j))],
        out_specs=pl.BlockSpec((g, tn), lambda j: (0, j)),
        out_shape=jax.ShapeDtypeStruct((g, n), F32),
        compiler_params=_cparams(("arbitrary",), 40),
        name="mod_params",
    )(cs, w_mod, b_mod.reshape(b_mod.shape[0], 1, n))


def _layer_norm(x):
    mu = jnp.mean(x, -1, keepdims=True)
    xc = x - mu
    var = jnp.mean(xc * xc, -1, keepdims=True)
    return xc * lax.rsqrt(var + EPS)


def _ln_mod_kernel(x_ref, sh_ref, sc_ref, o_ref):
    y = _layer_norm(x_ref[...])
    o_ref[...] = (y * (1.0 + sc_ref[...]) + sh_ref[...]).astype(o_ref.dtype)


def _resid_ln_kernel(x_ref, m_ref, gt_ref, g_ref, b_ref, *rest, alpha, with_mod):
    y = _layer_norm(alpha * x_ref[...] + gt_ref[...] * m_ref[...])
    xn = y * g_ref[...] + b_ref[...]
    if with_mod:
        sh_ref, sc_ref, xo_ref, ho_ref = rest
        xo_ref[...] = xn
        ho_ref[...] = (_layer_norm(xn) * (1.0 + sc_ref[...]) + sh_ref[...]).astype(ho_ref.dtype)
    else:
        (xo_ref,) = rest
        xo_ref[...] = xn


def _group_of_tile(i, lat_tiles, tiles_per_seq, n_batch):
    return jnp.where(i < lat_tiles, i // tiles_per_seq, n_batch)


class _Rows:
    def __init__(self, n_batch, n_lat, n_ctx):
        self.b, self.l, self.c = n_batch, n_lat, n_ctx
        self.t = n_batch * n_lat
        self.r = n_batch * (n_lat + n_ctx)
        self.lat_tiles = self.t // ROW_TILE
        self.lat_tps = n_lat // ROW_TILE
        self.ctx_tps = n_ctx // ROW_TILE

    def group(self, i):
        return _group_of_tile(i, self.lat_tiles, self.lat_tps, self.b)


def _mod_spec(rows, piece, d):
    return pl.BlockSpec((None, None, 1, d), lambda i: (piece, rows.group(i), 0, 0))


def _ln_mod(x, mods, rows, m, shift_piece, scale_piece, out_dtype):
    d = x.shape[1]
    return pl.pallas_call(
        _ln_mod_kernel,
        grid=(m // ROW_TILE,),
        in_specs=[pl.BlockSpec((ROW_TILE, d), lambda i: (i, 0)),
                  _mod_spec(rows, shift_piece, d), _mod_spec(rows, scale_piece, d)],
        out_specs=pl.BlockSpec((ROW_TILE, d), lambda i: (i, 0)),
        out_shape=jax.ShapeDtypeStruct((m, d), out_dtype),
        compiler_params=_cparams(("parallel",)),
        name="ln_mod",
    )(x, mods, mods)


def _resid_ln(x, mres, mods, gate_piece, ln_g, ln_b, rows, m, alpha, next_mods=None, next_dtype=BF16):
    d = x.shape[1]
    row_spec = pl.BlockSpec((ROW_TILE, d), lambda i: (i, 0))
    vec_spec = pl.BlockSpec((1, d), lambda i: (0, 0))
    in_specs = [row_spec, row_spec, _mod_spec(rows, gate_piece, d), vec_spec, vec_spec]
    args = [x, mres, mods, ln_g, ln_b]
    if next_mods is None:
        out_specs = row_spec
        out_shape = jax.ShapeDtypeStruct((m, d), F32)
    else:
        nm, sh_piece, sc_piece = next_mods
        in_specs += [_mod_spec(rows, sh_piece, d), _mod_spec(rows, sc_piece, d)]
        args += [nm, nm]
        out_specs = (row_spec, row_spec)
        out_shape = (jax.ShapeDtypeStruct((m, d), F32), jax.ShapeDtypeStruct((m, d), next_dtype))
    return pl.pallas_call(
        functools.partial(_resid_ln_kernel, alpha=alpha, with_mod=next_mods is not None),
        grid=(m // ROW_TILE,),
        in_specs=in_specs, out_specs=out_specs, out_shape=out_shape,
        compiler_params=_cparams(("parallel",)),
        name="resid_ln",
    )(*args)


def _mm_kernel(x_ref, w_ref, o_ref):
    o_ref[...] = _dot(x_ref[...], w_ref[...]).astype(o_ref.dtype)


def _mm(x, w, m, out_dtype, tm, tn, n_outer=False, vmem_mb=48):
    k, n = w.shape
    if n_outer:
        grid = (n // tn, m // tm)
        xi, wi, oi = (lambda j, i: (i, 0)), (lambda j, i: (0, j)), (lambda j, i: (i, j))
    else:
        grid = (m // tm, n // tn)
        xi, wi, oi = (lambda i, j: (i, 0)), (lambda i, j: (0, j)), (lambda i, j: (i, j))
    return pl.pallas_call(
        _mm_kernel,
        grid=grid,
        in_specs=[pl.BlockSpec((tm, k), xi), pl.BlockSpec((k, tn), wi)],
        out_specs=pl.BlockSpec((tm, tn), oi),
        out_shape=jax.ShapeDtypeStruct((m, n), out_dtype),
        compiler_params=_cparams(("parallel", "parallel"), vmem_mb),
        name="mm",
    )(x, w)


def _swiglu_up_kernel(x_ref, w1_ref, w3_ref, o_ref):
    x = x_ref[...]
    o_ref[...] = (_silu(_dot(x, w1_ref[...])) * _dot(x, w3_ref[...])).astype(o_ref.dtype)


def _swiglu_up(x, w1, w3, m, tm, tf):
    k, f = w1.shape
    return pl.pallas_call(
        _swiglu_up_kernel,
        grid=(m // tm, f // tf),
        in_specs=[pl.BlockSpec((tm, k), lambda i, j: (i, 0)),
                  pl.BlockSpec((k, tf), lambda i, j: (0, j)),
                  pl.BlockSpec((k, tf), lambda i, j: (0, j))],
        out_specs=pl.BlockSpec((tm, tf), lambda i, j: (i, j)),
        out_shape=jax.ShapeDtypeStruct((m, f), BF16),
        compiler_params=_cparams(("parallel", "parallel"), 48),
        name="swiglu_up",
    )(x, w1, w3)


def _merge_kernel(ya_ref, yb_ref, wa_ref, wb_ref, ga_ref, gb_ref, o_ref):
    a = _dot(ya_ref[...], wa_ref[...])
    b = _dot(yb_ref[...], wb_ref[...])
    o_ref[...] = (jax.nn.sigmoid(ga_ref[...]) * a + jax.nn.sigmoid(gb_ref[...]) * b).astype(o_ref.dtype)


def _merge(ya, yb, wa, wb, gates, m, tm, tn):
    k, n = wa.shape
    nb = n // tn
    return pl.pallas_call(
        _merge_kernel,
        grid=(nb, m // tm),
        in_specs=[pl.BlockSpec((tm, k), lambda j, i: (i, 0)),
                  pl.BlockSpec((tm, k), lambda j, i: (i, 0)),
                  pl.BlockSpec((k, tn), lambda j, i: (0, j)),
                  pl.BlockSpec((k, tn), lambda j, i: (0, j)),
                  pl.BlockSpec((tm, tn), lambda j, i: (i, j)),
                  pl.BlockSpec((tm, tn), lambda j, i: (i, nb + j))],
        out_specs=pl.BlockSpec((tm, tn), lambda j, i: (i, j)),
        out_shape=jax.ShapeDtypeStruct((m, n), BF16),
        compiler_params=_cparams(("parallel", "parallel"), 48),
        name="merge",
    )(ya, yb, wa, wb, gates, gates)


CONV_COLS = 1024
HALO = 8
INV_BLOCK = 16
assert CHUNK == 4 * INV_BLOCK


def _gdn_prep_kernel(x_ref, p_ref, n_ref, w_ref, o_ref, buf_ref, *, rows):
    i = pl.program_id(0)
    j = pl.program_id(1)
    is_lat = i < rows.lat_tiles
    pos = jnp.where(is_lat, i % rows.lat_tps, (i - rows.lat_tiles) % rows.ctx_tps)
    tps = jnp.where(is_lat, rows.lat_tps, rows.ctx_tps)
    buf_ref[0:HALO, :] = jnp.where(pos == 0, 0.0, p_ref[...])
    buf_ref[HALO:HALO + ROW_TILE, :] = x_ref[...]
    buf_ref[HALO + ROW_TILE:, :] = jnp.where(pos == tps - 1, 0.0, n_ref[...])
    base = HALO - CONV_K // 2
    acc = w_ref[0:1, :] * buf_ref[base:base + ROW_TILE, :]
    for t in range(1, CONV_K):
        acc = acc + w_ref[t:t + 1, :] * buf_ref[base + t:base + t + ROW_TILE, :]
    y = _silu(acc)
    kind = j // (QK_W // CONV_COLS)
    scale = jnp.where(kind == 0, HEAD_DIM ** -0.5, 1.0)
    for hh in range(CONV_COLS // HEAD_DIM):
        sl = slice(hh * HEAD_DIM, (hh + 1) * HEAD_DIM)
        seg = y[:, sl]
        ss = jnp.sum(seg * seg, -1, keepdims=True)
        nrm = seg * (lax.rsqrt(ss + EPS) * scale)
        o_ref[:, sl] = jnp.where(kind == 2, seg, nrm)


def _gdn_prep(qkv, conv_w, rows):
    r, n = qkv.shape
    hb = ROW_TILE // HALO
    last = r // HALO - 1
    return pl.pallas_call(
        functools.partial(_gdn_prep_kernel, rows=rows),
        grid=(r // ROW_TILE, n // CONV_COLS),
        in_specs=[pl.BlockSpec((ROW_TILE, CONV_COLS), lambda i, j: (i, j)),
                  pl.BlockSpec((HALO, CONV_COLS), lambda i, j: (jnp.maximum(i * hb - 1, 0), j)),
                  pl.BlockSpec((HALO, CONV_COLS), lambda i, j: (jnp.minimum((i + 1) * hb, last), j)),
                  pl.BlockSpec((CONV_K, CONV_COLS), lambda i, j: (0, j))],
        out_specs=pl.BlockSpec((ROW_TILE, CONV_COLS), lambda i, j: (i, j)),
        out_shape=jax.ShapeDtypeStruct((r, n), F32),
        scratch_shapes=[pltpu.VMEM((ROW_TILE + 2 * HALO, CONV_COLS), F32)],
        compiler_params=_cparams(("parallel", "parallel")),
        name="gdn_prep",
    )(qkv, qkv, qkv, conv_w)


def _softplus(x):
    return jnp.maximum(x, 0.0) + jnp.log(1.0 + jnp.exp(-jnp.abs(x)))


def _gdn_scan_kernel(q_ref, k_ref, v_ref, ab_ref, abt_ref, al_ref, dtb_ref, alt_ref, dtbt_ref,
                     o_ref, st_ref):
    d = pl.program_id(0)
    s = pl.program_id(2)
    fwd = d == 0

    @pl.when(s == 0)
    def _():
        st_ref[...] = jnp.zeros_like(st_ref)

    nh = HEADS
    ab = ab_ref[...]
    g_all = -jnp.exp(al_ref[...]) * _softplus(ab[:, 0:2 * nh] + dtb_ref[...])
    beta_all = jax.nn.sigmoid(ab[:, 2 * nh:4 * nh])
    g_d = jnp.where(fwd, g_all[:, 0:nh], g_all[:, nh:2 * nh])
    beta_d = jnp.where(fwd, beta_all[:, 0:nh], beta_all[:, nh:2 * nh])
    abt = abt_ref[...]
    gt_all = -jnp.exp(alt_ref[...]) * _softplus(abt[0:2 * nh, :] + dtbt_ref[...])
    gt_d = jnp.where(fwd, gt_all[0:nh, :], gt_all[nh:2 * nh, :])

    ri = lax.broadcasted_iota(jnp.int32, (CHUNK, CHUNK), 0)
    ci = lax.broadcasted_iota(jnp.int32, (CHUNK, CHUNK), 1)
    order = (ri - ci) * jnp.where(fwd, 1, -1)
    incl = order >= 0
    strict = order > 0
    incl_t = order <= 0
    eye = (ri == ci).astype(F32)
    shift = INV_BLOCK.bit_length() - 1
    blk_xor = lax.shift_right_logical(ri, shift) ^ lax.shift_right_logical(ci, shift)
    blk_a = blk_xor == 0
    off_b = blk_xor == 1
    off_c = blk_xor >= 2
    gc_col = jnp.dot(incl.astype(F32), g_d, preferred_element_type=F32, precision=HI)
    gc_row = jnp.dot(gt_d, incl_t.astype(F32), preferred_element_type=F32, precision=HI)
    gl_row = jnp.sum(g_d, axis=0, keepdims=True)

    hs = range(nh)
    sls = [slice(h * HEAD_DIM, (h + 1) * HEAD_DIM) for h in hs]
    gcc = [gc_col[:, h:h + 1] for h in hs]
    bc = [beta_d[:, h:h + 1] for h in hs]
    gl = [gl_row[:, h:h + 1] for h in hs]
    decay = [jnp.where(incl, jnp.exp(gcc[h] - gc_row[h:h + 1, :]), 0.0) for h in hs]
    kb = [k_ref[:, sls[h]] * bc[h] for h in hs]
    kkqk = [_dot_nt(jnp.concatenate([kb[h], q_ref[:, sls[h]]], axis=0).astype(BF16),
                    k_ref[:, sls[h]].astype(BF16)) for h in hs]
    lmat = [jnp.where(strict, kkqk[h][0:CHUNK] * decay[h], 0.0) for h in hs]
    a_intra = [(kkqk[h][CHUNK:] * decay[h]).astype(BF16) for h in hs]
    p = [-jnp.where(blk_a, lmat[h], 0.0) for h in hs]
    inv = [eye + p[h] for h in hs]
    for _ in range(INV_BLOCK.bit_length() - 2):
        p16 = [p[h].astype(BF16) for h in hs]
        p = [_dot(p16[h], p16[h]) for h in hs]
        inv = [inv[h] + _dot(inv[h].astype(BF16), p[h].astype(BF16)) for h in hs]
    for off in (off_b, off_c):
        inv16 = [inv[h].astype(BF16) for h in hs]
        t = [_dot(jnp.where(off, lmat[h], 0.0).astype(BF16), inv16[h]).astype(BF16) for h in hs]
        inv = [inv[h] - _dot(inv16[h], t[h]) for h in hs]
    eg = [jnp.exp(gcc[h]) for h in hs]
    uw = [_dot(inv[h].astype(BF16),
               jnp.concatenate([v_ref[:, sls[h]] * bc[h], kb[h] * eg[h]], axis=1).astype(BF16)) for h in hs]
    wq = [_dot(jnp.concatenate([uw[h][:, HEAD_DIM:], q_ref[:, sls[h]] * eg[h]], axis=0).astype(BF16),
               st_ref[h].astype(BF16)) for h in hs]
    v16 = [(uw[h][:, 0:HEAD_DIM] - wq[h][0:CHUNK]).astype(BF16) for h in hs]
    for h in hs:
        o_ref[:, sls[h]] = wq[h][CHUNK:] + _dot(a_intra[h], v16[h])
    for h in hs:
        k_dec = k_ref[:, sls[h]] * jnp.exp(gl[h] - gcc[h])
        st_ref[h] = st_ref[h] * jnp.exp(gl[h]) + _dot(k_dec.T.astype(BF16), v16[h])


def _gdn_scan(qkvn, small, abt3, a_log, dt_bias, rows, ab_block):
    r = qkvn.shape[0]
    ncc, ncl = rows.c // CHUNK, rows.l // CHUNK
    lat_blocks = rows.t // CHUNK

    def rb(d, b, s):
        cc = jnp.where(d == 0, s, ncc - 1 - s)
        lc = jnp.where(d == 0, s - ncc, ncc + ncl - 1 - s)
        return jnp.where(s < ncc, lat_blocks + b * ncc + cc, b * ncl + lc)

    w = QK_W
    al = a_log.reshape(1, 2 * HEADS)
    dtb = dt_bias.reshape(1, 2 * HEADS)
    return pl.pallas_call(
        _gdn_scan_kernel,
        grid=(2, rows.b, ncc + ncl),
        in_specs=[pl.BlockSpec((CHUNK, w), lambda d, b, s: (rb(d, b, s), 0)),
                  pl.BlockSpec((CHUNK, w), lambda d, b, s: (rb(d, b, s), 1)),
                  pl.BlockSpec((CHUNK, w), lambda d, b, s: (rb(d, b, s), 2)),
                  pl.BlockSpec((CHUNK, LANES), lambda d, b, s: (rb(d, b, s), ab_block)),
                  pl.BlockSpec((None, CHUNK, CHUNK), lambda d, b, s: (rb(d, b, s), 0, 0)),
                  pl.BlockSpec((1, 2 * HEADS), lambda d, b, s: (0, 0)),
                  pl.BlockSpec((1, 2 * HEADS), lambda d, b, s: (0, 0)),
                  pl.BlockSpec((2 * HEADS, 1), lambda d, b, s: (0, 0)),
                  pl.BlockSpec((2 * HEADS, 1), lambda d, b, s: (0, 0))],
        out_specs=pl.BlockSpec((None, CHUNK, w), lambda d, b, s: (d, rb(d, b, s), 0)),
        out_shape=jax.ShapeDtypeStruct((2, r, w), F32),
        scratch_shapes=[pltpu.VMEM((HEADS, HEAD_DIM, HEAD_DIM), F32)],
        compiler_params=_cparams(("parallel", "parallel", "arbitrary")),
        name="gdn_scan",
    )(qkvn, qkvn, qkvn, small, abt3, al, dtb, al.reshape(-1, 1), dtb.reshape(-1, 1))


def _gdn_out_kernel(o_ref, z_ref, g_ref, y_ref):
    o = o_ref[0] + o_ref[1]
    z = z_ref[...]
    for h in range(HEADS):
        sl = slice(h * HEAD_DIM, (h + 1) * HEAD_DIM)
        seg = o[:, sl]
        ms = jnp.mean(seg * seg, -1, keepdims=True)
        y_ref[:, sl] = (seg * lax.rsqrt(ms + EPS) * g_ref[...] * _silu(z[:, sl])).astype(y_ref.dtype)


def _gdn_out(o2, z, gdn_norm, m):
    w = QK_W
    return pl.pallas_call(
        _gdn_out_kernel,
        grid=(m // ROW_TILE,),
        in_specs=[pl.BlockSpec((2, ROW_TILE, w), lambda i: (0, i, 0)),
                  pl.BlockSpec((ROW_TILE, w), lambda i: (i, 0)),
                  pl.BlockSpec((1, HEAD_DIM), lambda i: (0, 0))],
        out_specs=pl.BlockSpec((ROW_TILE, w), lambda i: (i, 0)),
        out_shape=jax.ShapeDtypeStruct((m, w), BF16),
        compiler_params=_cparams(("parallel",)),
        name="gdn_out",
    )(o2, z, gdn_norm.reshape(1, HEAD_DIM))


ATTN_SUB = 256
Q_HEAD_W = 2 * HEAD_DIM
UQ_HEAD_COLS = 3 * HEAD_DIM


def _rms(x, g):
    return x * lax.rsqrt(jnp.mean(x * x, -1, keepdims=True) + EPS) * g


def _mla_q_kernel(x_ref, g_ref, w_ref, tab_ref, o_ref):
    xn = _rms(x_ref[...], g_ref[...]).astype(BF16)
    cos_s = tab_ref[:, 0:LANES]
    sin_s = tab_ref[:, LANES:2 * LANES]
    for h in range(HEADS):
        r = _dot(xn, w_ref[:, h * UQ_HEAD_COLS:(h + 1) * UQ_HEAD_COLS])
        o_ref[:, h * Q_HEAD_W:h * Q_HEAD_W + LANES] = (r[:, 0:LANES] * Q_SCALE).astype(o_ref.dtype)
        rope = r[:, LANES:2 * LANES] * cos_s + r[:, 2 * LANES:] * sin_s
        o_ref[:, h * Q_HEAD_W + LANES:(h + 1) * Q_HEAD_W] = rope.astype(o_ref.dtype)


def _mla_q(small, q_norm, w_ext, tab, m, tm):
    return pl.pallas_call(
        _mla_q_kernel,
        grid=(m // tm,),
        in_specs=[pl.BlockSpec((tm, Q_LORA), lambda i: (i, 0)),
                  pl.BlockSpec((1, Q_LORA), lambda i: (0, 0)),
                  pl.BlockSpec(w_ext.shape, lambda i: (0, 0)),
                  pl.BlockSpec((tm, 4 * LANES), lambda i: (i, 0))],
        out_specs=pl.BlockSpec((tm, HEADS * Q_HEAD_W), lambda i: (i, 0)),
        out_shape=jax.ShapeDtypeStruct((m, HEADS * Q_HEAD_W), BF16),
        compiler_params=_cparams(("parallel",), 48),
        name="mla_q",
    )(small, q_norm.reshape(1, Q_LORA), w_ext, tab)


def _mla_kv_kernel(x_ref, kr_ref, g_ref, w_ref, tab_ref, k_ref, v_ref):
    xn = _rms(x_ref[...], g_ref[...]).astype(BF16)
    kr = kr_ref[...]
    rope = kr[:, 0:LANES] * tab_ref[:, 2 * LANES:3 * LANES] + kr[:, LANES:] * tab_ref[:, 3 * LANES:]
    rope = rope.astype(k_ref.dtype)
    for h in range(HEADS):
        r = _dot(xn, w_ref[:, h * Q_HEAD_W:(h + 1) * Q_HEAD_W])
        k_ref[:, h * Q_HEAD_W:h * Q_HEAD_W + LANES] = r[:, 0:LANES].astype(k_ref.dtype)
        k_ref[:, h * Q_HEAD_W + LANES:(h + 1) * Q_HEAD_W] = rope
        v_ref[:, h * HEAD_DIM:(h + 1) * HEAD_DIM] = r[:, LANES:].astype(v_ref.dtype)


def _mla_kv(small, kv_norm, w_ukv, tab, m, tm):
    return pl.pallas_call(
        _mla_kv_kernel,
        grid=(m // tm,),
        in_specs=[pl.BlockSpec((tm, KV_LORA), lambda i: (i, 1)),
                  pl.BlockSpec((tm, 2 * LANES), lambda i: (i, 4)),
                  pl.BlockSpec((1, KV_LORA), lambda i: (0, 0)),
                  pl.BlockSpec(w_ukv.shape, lambda i: (0, 0)),
                  pl.BlockSpec((tm, 4 * LANES), lambda i: (i, 0))],
        out_specs=(pl.BlockSpec((tm, HEADS * Q_HEAD_W), lambda i: (i, 0)),
                   pl.BlockSpec((tm, HEADS * HEAD_DIM), lambda i: (i, 0))),
        out_shape=(jax.ShapeDtypeStruct((m, HEADS * Q_HEAD_W), BF16),
                   jax.ShapeDtypeStruct((m, HEADS * HEAD_DIM), BF16)),
        compiler_params=_cparams(("parallel",), 48),
        name="mla_kv",
    )(small, small, kv_norm.reshape(1, KV_LORA), w_ukv, tab)


def _attn_kernel(*refs, with_latent):
    if with_latent:
        q_ref, kl_ref, vl_ref, kc_ref, vc_ref, o_ref = refs
    else:
        q_ref, kc_ref, vc_ref, o_ref = refs
    subs = [slice(i, i + ATTN_SUB) for i in range(0, q_ref.shape[0], ATTN_SUB)]

    def scores(s):
        q = q_ref[s, :]
        return (_dot_nt(q, kc_ref[...]), _dot_nt(q, kl_ref[...]) if with_latent else None)

    def finish(s, sc, sl):
        m = jnp.max(sc, -1, keepdims=True)
        if with_latent:
            m = jnp.maximum(m, jnp.max(sl, -1, keepdims=True))
        pc = jnp.exp2(sc - m)
        den = jnp.sum(pc, -1, keepdims=True)
        acc = _dot(pc.astype(BF16), vc_ref[...])
        if with_latent:
            p_lat = jnp.exp2(sl - m)
            den = den + jnp.sum(p_lat, -1, keepdims=True)
            acc = acc + _dot(p_lat.astype(BF16), vl_ref[...])
        o_ref[s, :] = (acc / den).astype(o_ref.dtype)

    pending = None
    for s in subs:
        nxt = (s,) + scores(s)
        if pending is not None:
            finish(*pending)
        pending = nxt
    finish(*pending)


def _attention(qf, kf, vf, rows, latent_queries, tq):
    b, l, c = rows.b, rows.l, rows.c
    ctx0 = rows.t // c
    kc_spec = pl.BlockSpec((c, Q_HEAD_W), lambda bi, h, qi: (ctx0 + bi, h))
    vc_spec = pl.BlockSpec((c, HEAD_DIM), lambda bi, h, qi: (ctx0 + bi, h))
    if latent_queries:
        nq = l // tq
        q_spec = pl.BlockSpec((tq, Q_HEAD_W), lambda bi, h, qi: (bi * nq + qi, h))
        o_spec = pl.BlockSpec((tq, HEAD_DIM), lambda bi, h, qi: (bi * nq + qi, h))
        in_specs = [q_spec,
                    pl.BlockSpec((l, Q_HEAD_W), lambda bi, h, qi: (bi, h)),
                    pl.BlockSpec((l, HEAD_DIM), lambda bi, h, qi: (bi, h)),
                    kc_spec, vc_spec]
        args = (qf, kf, vf, kf, vf)
        m_out = rows.t
    else:
        nq = c // tq
        q0 = rows.t // tq
        q_spec = pl.BlockSpec((tq, Q_HEAD_W), lambda bi, h, qi: (q0 + bi * nq + qi, h))
        o_spec = pl.BlockSpec((tq, HEAD_DIM), lambda bi, h, qi: (bi * nq + qi, h))
        in_specs = [q_spec, kc_spec, vc_spec]
        args = (qf, kf, vf)
        m_out = rows.r - rows.t
    return pl.pallas_call(
        functools.partial(_attn_kernel, with_latent=latent_queries),
        grid=(b, HEADS, nq),
        in_specs=in_specs, out_specs=o_spec,
        out_shape=jax.ShapeDtypeStruct((m_out, HEADS * HEAD_DIM), BF16),
        compiler_params=_cparams(("parallel", "parallel", "arbitrary"), 48),
        name="mla_attn",
    )(*args)


def _router_kernel(h_ref, w_ref, ri_ref, pos_ref, cnt_ref, carry_ref, *, n_experts):
    i = pl.program_id(0)

    @pl.when(i == 0)
    def _():
        carry_ref[...] = jnp.zeros_like(carry_ref)

    logits = jnp.dot(h_ref[...], w_ref[...], preferred_element_type=F32, precision=HI)
    lane = lax.broadcasted_iota(jnp.int32, logits.shape, 1).astype(F32)
    neg = -jnp.inf
    logits = jnp.where(lane < n_experts, logits, neg)
    m1 = jnp.max(logits, -1, keepdims=True)
    i1 = jnp.min(jnp.where(logits == m1, lane, float(LANES)), -1, keepdims=True)
    sel1 = lane == i1
    rest = jnp.where(sel1, neg, logits)
    m2 = jnp.max(rest, -1, keepdims=True)
    i2 = jnp.min(jnp.where(rest == m2, lane, float(LANES)), -1, keepdims=True)
    sel2 = lane == i2
    e = jnp.exp(m2 - m1)
    w1 = 1.0 / (1.0 + e)
    w2 = e / (1.0 + e)
    ri_ref[...] = jnp.where(lane == 0, i1,
                            jnp.where(lane == 1, i2,
                                      jnp.where(lane == 2, w1, jnp.where(lane == 3, w2, 0.0))))
    onehot = jnp.where(sel1, 1.0, jnp.where(sel2, 1.0, 0.0))
    tile = logits.shape[0]
    ri_t = lax.broadcasted_iota(jnp.int32, (tile, tile), 0)
    ci_t = lax.broadcasted_iota(jnp.int32, (tile, tile), 1)
    before = (ci_t < ri_t).astype(BF16)
    pos_ref[...] = _dot(before, onehot.astype(BF16)) + carry_ref[0:1, :]
    carry_ref[0:1, :] = carry_ref[0:1, :] + jnp.sum(onehot, axis=0, keepdims=True)
    cnt_ref[...] = jnp.broadcast_to(carry_ref[0:1, :], cnt_ref.shape)


def _router(hf, router_w, n_experts):
    t, d = hf.shape
    wpad = jnp.zeros((d, LANES), F32).at[:, :n_experts].set(router_w)
    return pl.pallas_call(
        functools.partial(_router_kernel, n_experts=n_experts),
        grid=(t // ROW_TILE,),
        in_specs=[pl.BlockSpec((ROW_TILE, d), lambda i: (i, 0)),
                  pl.BlockSpec((d, LANES), lambda i: (0, 0))],
        out_specs=(pl.BlockSpec((ROW_TILE, LANES), lambda i: (i, 0)),
                   pl.BlockSpec((ROW_TILE, LANES), lambda i: (i, 0)),
                   pl.BlockSpec((8, LANES), lambda i: (0, 0))),
        out_shape=(jax.ShapeDtypeStruct((t, LANES), F32),
                   jax.ShapeDtypeStruct((t, LANES), F32),
                   jax.ShapeDtypeStruct((8, LANES), F32)),
        scratch_shapes=[pltpu.VMEM((8, LANES), F32)],
        compiler_params=_cparams(("arbitrary",)),
        name="moe_router",
    )(hf, wpad)


def _row_copy(src_hbm, row, buf, r, sem):
    return pltpu.make_async_copy(src_hbm.at[pl.ds(row, 1), :], buf.at[pl.ds(r, 1), :], sem)


GATHER_UNROLL = 8


def _gather_rows(idx_ref, base, n, src_hbm, buf, sem):
    def issue(r, c):
        _row_copy(src_hbm, idx_ref[base + r], buf, r, sem).start()
        return c

    lax.fori_loop(0, n, issue, 0, unroll=GATHER_UNROLL)
    pltpu.make_async_copy(src_hbm.at[pl.ds(0, n), :], buf, sem).wait()


def _moe_gather_kernel(valid_ref, tok_ref, h_hbm, o_ref, buf, sem):
    i = pl.program_id(0)

    @pl.when(valid_ref[i] > 0)
    def _():
        _gather_rows(tok_ref, i * MOE_TILE, MOE_TILE, h_hbm, buf, sem)
        o_ref[...] = buf[...].astype(o_ref.dtype)

    @pl.when(valid_ref[i] == 0)
    def _():
        o_ref[...] = jnp.zeros_like(o_ref)


def _moe_gather(valid, tok, hf, n_slots):
    d = hf.shape[1]
    return pl.pallas_call(
        _moe_gather_kernel,
        grid_spec=pltpu.PrefetchScalarGridSpec(
            num_scalar_prefetch=2,
            grid=(n_slots // MOE_TILE,),
            in_specs=[pl.BlockSpec(memory_space=pl.ANY)],
            out_specs=pl.BlockSpec((MOE_TILE, d), lambda i, v, t: (i, 0)),
            scratch_shapes=[pltpu.VMEM((MOE_TILE, d), F32), pltpu.SemaphoreType.DMA]),
        out_shape=jax.ShapeDtypeStruct((n_slots, d), BF16),
        compiler_params=_cparams(("arbitrary",)),
        name="moe_gather",
    )(valid, tok, hf)


def _moe_up_kernel(te_ref, valid_ref, x_ref, w1_ref, w3_ref, o_ref, w1b_ref, w3b_ref):
    i = pl.program_id(1)

    @pl.when((i == 0) | (te_ref[i] != te_ref[jnp.maximum(i - 1, 0)]))
    def _():
        w1b_ref[...] = w1_ref[...].astype(BF16)
        w3b_ref[...] = w3_ref[...].astype(BF16)

    @pl.when(valid_ref[i] > 0)
    def _():
        x = x_ref[...]
        o_ref[...] = (_silu(_dot(x, w1b_ref[...])) * _dot(x, w3b_ref[...])).astype(o_ref.dtype)

    @pl.when(valid_ref[i] == 0)
    def _():
        o_ref[...] = jnp.zeros_like(o_ref)


def _moe_up(te, valid, xs, w1, w3, layer, tf):
    s, d = xs.shape
    f = w1.shape[3]
    w_spec = pl.BlockSpec((None, None, d, tf), lambda j, i, te, v: (layer, te[i], 0, j))
    return pl.pallas_call(
        _moe_up_kernel,
        grid_spec=pltpu.PrefetchScalarGridSpec(
            num_scalar_prefetch=2,
            grid=(f // tf, s // MOE_TILE),
            in_specs=[pl.BlockSpec((MOE_TILE, d), lambda j, i, te, v: (i, 0)), w_spec, w_spec],
            out_specs=pl.BlockSpec((MOE_TILE, tf), lambda j, i, te, v: (i, j)),
            scratch_shapes=[pltpu.VMEM((d, tf), BF16), pltpu.VMEM((d, tf), BF16)]),
        out_shape=jax.ShapeDtypeStruct((s, f), BF16),
        compiler_params=_cparams(("arbitrary", "arbitrary"), 48),
        name="moe_up",
    )(te, valid, xs, w1, w3)


def _moe_down_kernel(te_ref, valid_ref, x_ref, w_ref, o_ref):
    i = pl.program_id(1)

    @pl.when(valid_ref[i] > 0)
    def _():
        o_ref[...] = _dot(x_ref[...], w_ref[...])

    @pl.when(valid_ref[i] == 0)
    def _():
        o_ref[...] = jnp.zeros_like(o_ref)


def _moe_down(te, valid, hmid, w2, tn):
    s, f = hmid.shape
    d = w2.shape[2]
    return pl.pallas_call(
        _moe_down_kernel,
        grid_spec=pltpu.PrefetchScalarGridSpec(
            num_scalar_prefetch=2,
            grid=(d // tn, s // MOE_TILE),
            in_specs=[pl.BlockSpec((MOE_TILE, f), lambda j, i, te, v: (i, 0)),
                      pl.BlockSpec((None, f, tn), lambda j, i, te, v: (te[i], 0, j))],
            out_specs=pl.BlockSpec((MOE_TILE, tn), lambda j, i, te, v: (i, j))),
        out_shape=jax.ShapeDtypeStruct((s, d), F32),
        compiler_params=_cparams(("arbitrary", "arbitrary"), 48),
        name="moe_down",
    )(te, valid, hmid, w2)


def _moe_combine_kernel(s1_ref, s2_ref, y_hbm, ri_ref, x_ref, gt_ref, g_ref, b_ref, o_ref,
                        buf1, buf2, sem1, sem2, *, alpha):
    i = pl.program_id(0)
    base = i * ROW_TILE

    def issue(r, c):
        _row_copy(y_hbm, s1_ref[base + r], buf1, r, sem1).start()
        _row_copy(y_hbm, s2_ref[base + r], buf2, r, sem2).start()
        return c

    lax.fori_loop(0, ROW_TILE, issue, 0, unroll=GATHER_UNROLL)
    pltpu.make_async_copy(y_hbm.at[pl.ds(0, ROW_TILE), :], buf1, sem1).wait()
    pltpu.make_async_copy(y_hbm.at[pl.ds(0, ROW_TILE), :], buf2, sem2).wait()
    ri = ri_ref[...]
    y = ri[:, 2:3] * buf1[...] + ri[:, 3:4] * buf2[...]
    xn = _layer_norm(alpha * x_ref[...] + gt_ref[...] * y)
    o_ref[...] = xn * g_ref[...] + b_ref[...]


def _moe_combine(slot1, slot2, yslot, ri, x, mods, gate_piece, ln_g, ln_b, rows, alpha):
    t, d = x.shape
    row_spec = pl.BlockSpec((ROW_TILE, d), lambda i, a, b: (i, 0))
    vec_spec = pl.BlockSpec((1, d), lambda i, a, b: (0, 0))
    return pl.pallas_call(
        functools.partial(_moe_combine_kernel, alpha=alpha),
        grid_spec=pltpu.PrefetchScalarGridSpec(
            num_scalar_prefetch=2,
            grid=(t // ROW_TILE,),
            in_specs=[pl.BlockSpec(memory_space=pl.ANY),
                      pl.BlockSpec((ROW_TILE, LANES), lambda i, a, b: (i, 0)),
                      row_spec,
                      pl.BlockSpec((None, None, 1, d), lambda i, a, b: (gate_piece, rows.group(i), 0, 0)),
                      vec_spec, vec_spec],
            out_specs=row_spec,
            scratch_shapes=[pltpu.VMEM((ROW_TILE, d), F32), pltpu.VMEM((ROW_TILE, d), F32),
                            pltpu.SemaphoreType.DMA, pltpu.SemaphoreType.DMA]),
        out_shape=jax.ShapeDtypeStruct((t, d), F32),
        compiler_params=_cparams(("arbitrary",)),
        name="moe_combine",
    )(slot1, slot2, yslot, ri, x, mods, ln_g, ln_b)


def _moe_plan(ri, pos, cnt, n_experts, n_slots):
    t = ri.shape[0]
    i1 = ri[:, 0].astype(jnp.int32)
    i2 = ri[:, 1].astype(jnp.int32)
    pos8 = pos[:, :n_experts].astype(jnp.int32)
    counts = cnt[0, :n_experts].astype(jnp.int32)
    padded = ((counts + MOE_TILE - 1) // MOE_TILE) * MOE_TILE
    ends = jnp.cumsum(padded)
    offs = ends - padded
    slot1 = offs[i1] + jnp.take_along_axis(pos8, i1[:, None], axis=1)[:, 0]
    slot2 = offs[i2] + jnp.take_along_axis(pos8, i2[:, None], axis=1)[:, 0]
    ids = jnp.arange(t, dtype=jnp.int32)
    tok = jnp.zeros((n_slots,), jnp.int32).at[slot1].set(ids).at[slot2].set(ids)
    starts = jnp.arange(n_slots // MOE_TILE, dtype=jnp.int32) * MOE_TILE
    valid = (starts < ends[-1]).astype(jnp.int32)
    te = jnp.minimum(jnp.searchsorted(ends, starts, side="right").astype(jnp.int32), n_experts - 1)
    te = jnp.where(valid > 0, te, jnp.max(jnp.where(valid > 0, te, 0)))
    return slot1, slot2, tok, te, valid


def _rope_tables(rows):
    l = rows.l
    nrow = l // GRID_W
    row = jnp.repeat(jnp.arange(nrow), GRID_W).astype(F32)
    col = jnp.tile(jnp.arange(GRID_W), nrow).astype(F32)
    n_freq = ROPE_DIM // 4
    inv_freq = ROPE_THETA ** (-jnp.arange(n_freq, dtype=F32) / n_freq)
    cr, sr = jnp.cos(row[:, None] * inv_freq), jnp.sin(row[:, None] * inv_freq)
    cc, sc = jnp.cos(col[:, None] * inv_freq), jnp.sin(col[:, None] * inv_freq)
    cos = jnp.concatenate([cr, cr, cc, cc], -1)
    sin = jnp.concatenate([-sr, sr, -sc, sc], -1)
    nctx = rows.r - rows.t
    cos = jnp.concatenate([jnp.tile(cos, (rows.b, 1)), jnp.ones((nctx, ROPE_DIM), F32)], 0)
    sin = jnp.concatenate([jnp.tile(sin, (rows.b, 1)), jnp.zeros((nctx, ROPE_DIM), F32)], 0)
    z = jnp.zeros((rows.r, LANES - ROPE_DIM), F32)
    return jnp.concatenate([Q_SCALE * cos, z, Q_SCALE * sin, z, cos, z, sin, z], -1)


_ROPE_SWAP = np.concatenate([np.arange(16, 32), np.arange(0, 16), np.arange(48, 64), np.arange(32, 48)])


def _uq_ext(w_uq):
    k = w_uq.shape[0]
    w = w_uq.reshape(k, HEADS, HEAD_DIM + ROPE_DIM)
    nope, rope = w[:, :, :HEAD_DIM], w[:, :, HEAD_DIM:]
    z = jnp.zeros((k, HEADS, LANES - ROPE_DIM), w.dtype)
    ext = jnp.concatenate([nope, rope, z, rope[:, :, _ROPE_SWAP], z], -1)
    return ext.reshape(k, HEADS * UQ_HEAD_COLS).astype(BF16)


def kernel(x, c, ctx, c_ctx, w_mod, b_mod, w_in, conv_w, a_log, dt_bias, gdn_norm, q_norm, kv_norm, w_uq, w_ukv, w_br_a, w_br_b, w_out, ln1_g, ln1_b, ln2_g, ln2_b, ffn_w1, ffn_w3, ffn_w2, moe_router, moe_w1, moe_w3, moe_w2):
    bsz, n_lat, d = x.shape
    n_ctx = ctx.shape[1]
    depth = w_mod.shape[0]
    assert depth == 2 and ffn_w1.shape[0] == 1 and moe_w1.shape[0] == 1
    rows = _Rows(bsz, n_lat, n_ctx)
    r_all, t_lat = rows.r, rows.t
    alpha = (2 * depth) ** 0.25
    n_experts = moe_router.shape[2]

    xs = jnp.concatenate([x.reshape(t_lat, d), ctx.reshape(r_all - t_lat, d)], 0)
    n_groups = -(-(bsz + 1) // 8) * 8
    cs = jnp.zeros((n_groups, d), F32).at[:bsz].set(c).at[bsz].set(c_ctx)
    tab = _rope_tables(rows)

    o_z = 3 * QK_W
    o_a = o_z + QK_W
    o_b = o_a + 2 * HEADS
    o_dq = o_b + 2 * HEADS
    o_dkv = o_dq + Q_LORA
    o_kr = o_dkv + KV_LORA
    o_ga = o_kr + ROPE_DIM
    ab_block = (Q_LORA + KV_LORA + 2 * LANES) // LANES

    tm_all = _pick(r_all, (1024, 512, 256))
    tm_lat = _pick(t_lat, (1024, 512, 256))
    tm_half = _pick(r_all, (512, 256))

    all_mods = [_mod_params(cs, w_mod, b_mod, i).reshape(n_groups, 6, d).transpose(1, 0, 2)[:, :, None, :]
                for i in range(depth)]

    h = None
    for i in range(depth):
        last = i == depth - 1
        m_mix = t_lat if last else r_all
        tm_mix = tm_lat if last else tm_all
        mods = all_mods[i]
        if i == 0:
            h = _ln_mod(xs, mods, rows, r_all, 0, 1, BF16)

        wi = w_in[i]
        kr_w = wi[:, o_kr:o_ga]
        zc = jnp.zeros((d, LANES - ROPE_DIM), F32)
        w_small = jnp.concatenate(
            [wi[:, o_dq:o_kr], kr_w, zc, kr_w[:, _ROPE_SWAP], zc, wi[:, o_a:o_dq], zc], 1).astype(BF16)
        qkv = _mm(h, wi[:, :o_z].astype(BF16), r_all, F32, tm_all, _pick(o_z, (1536, 1024, 512)))
        small = _mm(h, w_small, r_all, F32, tm_all, w_small.shape[1])
        z = _mm(h, wi[:, o_z:o_a].astype(BF16), m_mix, F32, tm_mix, 1024)
        gates = _mm(h, wi[:, o_ga:].astype(BF16), m_mix, F32, tm_mix, 1024)

        qkvn = _gdn_prep(qkv, conv_w[i], rows)
        ab = small[:, ab_block * LANES:ab_block * LANES + 4 * HEADS]
        abt3 = ab.reshape(r_all // CHUNK, CHUNK, 4 * HEADS).transpose(0, 2, 1)
        o2 = _gdn_scan(qkvn, small, abt3, a_log[i], dt_bias[i], rows, ab_block)
        ya = _gdn_out(o2, z, gdn_norm[i], m_mix)

        qf = _mla_q(small, q_norm[i], _uq_ext(w_uq[i]), tab, m_mix, _pick(m_mix, (512, 256)))
        kf, vf = _mla_kv(small, kv_norm[i], w_ukv[i].astype(BF16), tab, r_all, tm_half)
        yb = _attention(qf, kf, vf, rows, True, _pick(n_lat, (512, 256)))
        if not last:
            yb_ctx = _attention(qf, kf, vf, rows, False, _pick(n_ctx, (256,)))
            yb = jnp.concatenate([yb, yb_ctx], 0)

        ym = _merge(ya, yb, w_br_a[i].astype(BF16), w_br_b[i].astype(BF16), gates, m_mix,
                    _pick(m_mix, (512, 256)), 1024)
        mres = _mm(ym, w_out[i].astype(BF16), m_mix, F32, tm_mix, 1024)
        moe_layer = i % 2 == 1
        x1, h2 = _resid_ln(xs, mres, mods, 2, ln1_g[i].reshape(1, d), ln1_b[i].reshape(1, d), rows, m_mix,
                           alpha, next_mods=(mods, 3, 4), next_dtype=F32 if moe_layer else BF16)

        g2, b2 = ln2_g[i].reshape(1, d), ln2_b[i].reshape(1, d)
        if not moe_layer:
            j = i // 2
            hmid = _swiglu_up(h2, ffn_w1[j].astype(BF16), ffn_w3[j].astype(BF16), m_mix, tm_mix, 512)
            f_out = _mm(hmid, ffn_w2[j].astype(BF16), m_mix, F32, _pick(m_mix, (512, 256)), 1024, n_outer=True)
            if last:
                xs = _resid_ln(x1, f_out, mods, 5, g2, b2, rows, m_mix, alpha)
            else:
                xs, h = _resid_ln(x1, f_out, mods, 5, g2, b2, rows, m_mix, alpha,
                                  next_mods=(all_mods[i + 1], 0, 1))
        else:
            assert last
            j = i // 2
            ri, pos, cnt = _router(h2, moe_router[j], n_experts)
            n_slots = TOP_K * t_lat + n_experts * MOE_TILE
            slot1, slot2, tok, te, valid = _moe_plan(ri, pos, cnt, n_experts, n_slots)
            xg = _moe_gather(valid, tok, h2, n_slots)
            hmid = _moe_up(te, valid, xg, moe_w1, moe_w3, j, 512)
            yslot = _moe_down(te, valid, hmid, moe_w2[j].astype(BF16), 512)
            xs = _moe_combine(slot1, slot2, yslot, ri, x1, mods, 5, g2, b2, rows, alpha)
    return xs[:t_lat].reshape(bsz, n_lat, d)
```

```python
import functools
import math

import numpy as np
import jax
import jax.numpy as jnp
from jax import lax
from jax.experimental import pallas as pl
from jax.experimental.pallas import tpu as pltpu

F32 = jnp.float32
BF16 = jnp.bfloat16
HI = lax.Precision.HIGHEST

HEADS = 16
HEAD_DIM = 128
ROPE_DIM = 64
Q_LORA = 512
KV_LORA = 512
CHUNK = 64
CONV_K = 5
GRID_W = 64
ROPE_THETA = 10000.0
QK_W = HEADS * HEAD_DIM
MLA_SCALE = (HEAD_DIM + ROPE_DIM) ** -0.5
Q_SCALE = MLA_SCALE * math.log2(math.e)
TOP_K = 2
EPS = 1e-6
ROW_TILE = 256
MOE_TILE = 512
LANES = 128


def _cparams(sem, vmem_mb=None):
    kw = dict(dimension_semantics=sem)
    if vmem_mb is not None:
        kw["vmem_limit_bytes"] = vmem_mb << 20
    return pltpu.CompilerParams(**kw)


def _pick(m, cands):
    for c in cands:
        if m % c == 0:
            return c
    raise ValueError(f"no tile for {m}")


def _dot(a, b):
    return jnp.dot(a, b, preferred_element_type=F32)


def _dot_nt(a, b):
    return lax.dot_general(a, b, (((1,), (1,)), ((), ())), preferred_element_type=F32)


def _silu(x):
    return x * jax.nn.sigmoid(x)


def _mod_kernel(c_ref, w_ref, b_ref, o_ref):
    s = _silu(c_ref[...])
    o_ref[...] = jnp.dot(s, w_ref[...], preferred_element_type=F32, precision=HI) + b_ref[...]


def _mod_params(cs, w_mod, b_mod, layer):
    g, d = cs.shape
    n = w_mod.shape[2]
    tn = _pick(n, (1024, 512))
    return pl.pallas_call(
        _mod_kernel,
        grid=(n // tn,),
        in_specs=[pl.BlockSpec((g, d), lambda j: (0, 0)),
                  pl.BlockSpec((None, d, tn), lambda j: (layer, 0, j)),
                  pl.BlockSpec((None, 1, tn), lambda j: (layer, 0, j))],
        out_specs=pl.BlockSpec((g, tn), lambda j: (0, j)),
        out_shape=jax.ShapeDtypeStruct((g, n), F32),
        compiler_params=_cparams(("arbitrary",), 40),
        name="mod_params",
    )(cs, w_mod, b_mod.reshape(b_mod.shape[0], 1, n))


def _layer_norm(x):
    mu = jnp.mean(x, -1, keepdims=True)
    xc = x - mu
    var = jnp.mean(xc * xc, -1, keepdims=True)
    return xc * lax.rsqrt(var + EPS)


def _ln_mod_kernel(x_ref, sh_ref, sc_ref, o_ref):
    y = _layer_norm(x_ref[...])
    o_ref[...] = (y * (1.0 + sc_ref[...]) + sh_ref[...]).astype(o_ref.dtype)


def _resid_ln_kernel(x_ref, m_ref, gt_ref, g_ref, b_ref, *rest, alpha, with_mod):
    y = _layer_norm(alpha * x_ref[...] + gt_ref[...] * m_ref[...])
    xn = y * g_ref[...] + b_ref[...]
    if with_mod:
        sh_ref, sc_ref, xo_ref, ho_ref = rest
        xo_ref[...] = xn
        ho_ref[...] = (_layer_norm(xn) * (1.0 + sc_ref[...]) + sh_ref[...]).astype(ho_ref.dtype)
    else:
        (xo_ref,) = rest
        xo_ref[...] = xn


def _group_of_tile(i, lat_tiles, tiles_per_seq, n_batch):
    return jnp.where(i < lat_tiles, i // tiles_per_seq, n_batch)


class _Rows:
    def __init__(self, n_batch, n_lat, n_ctx):
        self.b, self.l, self.c = n_batch, n_lat, n_ctx
        self.t = n_batch * n_lat
        self.r = n_batch * (n_lat + n_ctx)
        self.lat_tiles = self.t // ROW_TILE
        self.lat_tps = n_lat // ROW_TILE
        self.ctx_tps = n_ctx // ROW_TILE

    def group(self, i):
        return _group_of_tile(i, self.lat_tiles, self.lat_tps, self.b)


def _mod_spec(rows, piece, d):
    return pl.BlockSpec((None, None, 1, d), lambda i: (piece, rows.group(i), 0, 0))


def _ln_mod(x, mods, rows, m, shift_piece, scale_piece, out_dtype):
    d = x.shape[1]
    return pl.pallas_call(
        _ln_mod_kernel,
        grid=(m // ROW_TILE,),
        in_specs=[pl.BlockSpec((ROW_TILE, d), lambda i: (i, 0)),
                  _mod_spec(rows, shift_piece, d), _mod_spec(rows, scale_piece, d)],
        out_specs=pl.BlockSpec((ROW_TILE, d), lambda i: (i, 0)),
        out_shape=jax.ShapeDtypeStruct((m, d), out_dtype),
        compiler_params=_cparams(("parallel",)),
        name="ln_mod",
    )(x, mods, mods)


def _resid_ln(x, mres, mods, gate_piece, ln_g, ln_b, rows, m, alpha, next_mods=None, next_dtype=BF16):
    d = x.shape[1]
    row_spec = pl.BlockSpec((ROW_TILE, d), lambda i: (i, 0))
    vec_spec = pl.BlockSpec((1, d), lambda i: (0, 0))
    in_specs = [row_spec, row_spec, _mod_spec(rows, gate_piece, d), vec_spec, vec_spec]
    args = [x, mres, mods, ln_g, ln_b]
    if next_mods is None:
        out_specs = row_spec
        out_shape = jax.ShapeDtypeStruct((m, d), F32)
    else:
        nm, sh_piece, sc_piece = next_mods
        in_specs += [_mod_spec(rows, sh_piece, d), _mod_spec(rows, sc_piece, d)]
        args += [nm, nm]
        out_specs = (row_spec, row_spec)
        out_shape = (jax.ShapeDtypeStruct((m, d), F32), jax.ShapeDtypeStruct((m, d), next_dtype))
    return pl.pallas_call(
        functools.partial(_resid_ln_kernel, alpha=alpha, with_mod=next_mods is not None),
        grid=(m // ROW_TILE,),
        in_specs=in_specs, out_specs=out_specs, out_shape=out_shape,
        compiler_params=_cparams(("parallel",)),
        name="resid_ln",
    )(*args)


def _mm_kernel(x_ref, w_ref, o_ref):
    o_ref[...] = _dot(x_ref[...], w_ref[...]).astype(o_ref.dtype)


def _mm(x, w, m, out_dtype, tm, tn, n_outer=False, vmem_mb=48):
    k, n = w.shape
    if n_outer:
        grid = (n // tn, m // tm)
        xi, wi, oi = (lambda j, i: (i, 0)), (lambda j, i: (0, j)), (lambda j, i: (i, j))
    else:
        grid = (m // tm, n // tn)
        xi, wi, oi = (lambda i, j: (i, 0)), (lambda i, j: (0, j)), (lambda i, j: (i, j))
    return pl.pallas_call(
        _mm_kernel,
        grid=grid,
        in_specs=[pl.BlockSpec((tm, k), xi), pl.BlockSpec((k, tn), wi)],
        out_specs=pl.BlockSpec((tm, tn), oi),
        out_shape=jax.ShapeDtypeStruct((m, n), out_dtype),
        compiler_params=_cparams(("parallel", "parallel"), vmem_mb),
        name="mm",
    )(x, w)


def _swiglu_up_kernel(x_ref, w1_ref, w3_ref, o_ref):
    x = x_ref[...]
    o_ref[...] = (_silu(_dot(x, w1_ref[...])) * _dot(x, w3_ref[...])).astype(o_ref.dtype)


def _swiglu_up(x, w1, w3, m, tm, tf):
    k, f = w1.shape
    return pl.pallas_call(
        _swiglu_up_kernel,
        grid=(m // tm, f // tf),
        in_specs=[pl.BlockSpec((tm, k), lambda i, j: (i, 0)),
                  pl.BlockSpec((k, tf), lambda i, j: (0, j)),
                  pl.BlockSpec((k, tf), lambda i, j: (0, j))],
        out_specs=pl.BlockSpec((tm, tf), lambda i, j: (i, j)),
        out_shape=jax.ShapeDtypeStruct((m, f), BF16),
        compiler_params=_cparams(("parallel", "parallel"), 48),
        name="swiglu_up",
    )(x, w1, w3)


def _merge_kernel(ya_ref, yb_ref, wa_ref, wb_ref, ga_ref, gb_ref, o_ref):
    a = _dot(ya_ref[...], wa_ref[...])
    b = _dot(yb_ref[...], wb_ref[...])
    o_ref[...] = (jax.nn.sigmoid(ga_ref[...]) * a + jax.nn.sigmoid(gb_ref[...]) * b).astype(o_ref.dtype)


def _merge(ya, yb, wa, wb, gates, m, tm, tn):
    k, n = wa.shape
    nb = n // tn
    return pl.pallas_call(
        _merge_kernel,
        grid=(nb, m // tm),
        in_specs=[pl.BlockSpec((tm, k), lambda j, i: (i, 0)),
                  pl.BlockSpec((tm, k), lambda j, i: (i, 0)),
                  pl.BlockSpec((k, tn), lambda j, i: (0, j)),
                  pl.BlockSpec((k, tn), lambda j, i: (0, j)),
                  pl.BlockSpec((tm, tn), lambda j, i: (i, j)),
                  pl.BlockSpec((tm, tn), lambda j, i: (i, nb + j))],
        out_specs=pl.BlockSpec((tm, tn), lambda j, i: (i, j)),
        out_shape=jax.ShapeDtypeStruct((m, n), BF16),
        compiler_params=_cparams(("parallel", "parallel"), 48),
        name="merge",
    )(ya, yb, wa, wb, gates, gates)


CONV_COLS = 1024
HALO = 8
INV_BLOCK = 16
assert CHUNK == 4 * INV_BLOCK


def _gdn_prep_kernel(x_ref, p_ref, n_ref, w_ref, o_ref, buf_ref, *, rows):
    i = pl.program_id(0)
    j = pl.program_id(1)
    is_lat = i < rows.lat_tiles
    pos = jnp.where(is_lat, i % rows.lat_tps, (i - rows.lat_tiles) % rows.ctx_tps)
    tps = jnp.where(is_lat, rows.lat_tps, rows.ctx_tps)
    buf_ref[0:HALO, :] = jnp.where(pos == 0, 0.0, p_ref[...])
    buf_ref[HALO:HALO + ROW_TILE, :] = x_ref[...]
    buf_ref[HALO + ROW_TILE:, :] = jnp.where(pos == tps - 1, 0.0, n_ref[...])
    base = HALO - CONV_K // 2
    acc = w_ref[0:1, :] * buf_ref[base:base + ROW_TILE, :]
    for t in range(1, CONV_K):
        acc = acc + w_ref[t:t + 1, :] * buf_ref[base + t:base + t + ROW_TILE, :]
    y = _silu(acc)
    kind = j // (QK_W // CONV_COLS)
    scale = jnp.where(kind == 0, HEAD_DIM ** -0.5, 1.0)
    for hh in range(CONV_COLS // HEAD_DIM):
        sl = slice(hh * HEAD_DIM, (hh + 1) * HEAD_DIM)
        seg = y[:, sl]
        ss = jnp.sum(seg * seg, -1, keepdims=True)
        nrm = seg * (lax.rsqrt(ss + EPS) * scale)
        o_ref[:, sl] = jnp.where(kind == 2, seg, nrm)


def _gdn_prep(qkv, conv_w, rows):
    r, n = qkv.shape
    hb = ROW_TILE // HALO
    last = r // HALO - 1
    return pl.pallas_call(
        functools.partial(_gdn_prep_kernel, rows=rows),
        grid=(r // ROW_TILE, n // CONV_COLS),
        in_specs=[pl.BlockSpec((ROW_TILE, CONV_COLS), lambda i, j: (i, j)),
                  pl.BlockSpec((HALO, CONV_COLS), lambda i, j: (jnp.maximum(i * hb - 1, 0), j)),
                  pl.BlockSpec((HALO, CONV_COLS), lambda i, j: (jnp.minimum((i + 1) * hb, last), j)),
                  pl.BlockSpec((CONV_K, CONV_COLS), lambda i, j: (0, j))],
        out_specs=pl.BlockSpec((ROW_TILE, CONV_COLS), lambda i, j: (i, j)),
        out_shape=jax.ShapeDtypeStruct((r, n), F32),
        scratch_shapes=[pltpu.VMEM((ROW_TILE + 2 * HALO, CONV_COLS), F32)],
        compiler_params=_cparams(("parallel", "parallel")),
        name="gdn_prep",
    )(qkv, qkv, qkv, conv_w)


def _softplus(x):
    return jnp.maximum(x, 0.0) + jnp.log(1.0 + jnp.exp(-jnp.abs(x)))


def _gdn_scan_kernel(qf_ref, kf_ref, vf_ref, abf_ref, abtf_ref, qb_ref, kb_ref, vb_ref, abb_ref, abtb_ref,
                     al_ref, dtb_ref, alt_ref, dtbt_ref, of_ref, ob_ref, st_ref):
    @pl.when(pl.program_id(1) == 0)
    def _():
        st_ref[...] = jnp.zeros_like(st_ref)

    shared = (al_ref, dtb_ref, alt_ref, dtbt_ref)
    pending = [_scan_direction(True, qf_ref, kf_ref, vf_ref, abf_ref, abtf_ref, *shared, of_ref, st_ref.at[0]),
               _scan_direction(False, qb_ref, kb_ref, vb_ref, abb_ref, abtb_ref, *shared, ob_ref, st_ref.at[1])]
    while pending:
        for gen in list(pending):
            try:
                next(gen)
            except StopIteration:
                pending.remove(gen)


def _scan_direction(fwd, q_ref, k_ref, v_ref, ab_ref, abt_ref, al_ref, dtb_ref, alt_ref, dtbt_ref, o_ref, st_ref):
    nh = HEADS
    dcols = slice(0, nh) if fwd else slice(nh, 2 * nh)
    ab = ab_ref[...]
    g_all = -jnp.exp(al_ref[...]) * _softplus(ab[:, 0:2 * nh] + dtb_ref[...])
    g_d = g_all[:, dcols]
    beta_d = jax.nn.sigmoid(ab[:, 2 * nh:4 * nh])[:, dcols]
    abt = abt_ref[...]
    gt_d = (-jnp.exp(alt_ref[...]) * _softplus(abt[0:2 * nh, :] + dtbt_ref[...]))[dcols, :]

    ri = lax.broadcasted_iota(jnp.int32, (CHUNK, CHUNK), 0)
    ci = lax.broadcasted_iota(jnp.int32, (CHUNK, CHUNK), 1)
    order = (ri - ci) if fwd else (ci - ri)
    incl = order >= 0
    strict = order > 0
    eye = (ri == ci).astype(F32)
    shift = INV_BLOCK.bit_length() - 1
    blk_xor = lax.shift_right_logical(ri, shift) ^ lax.shift_right_logical(ci, shift)
    blk_a = blk_xor == 0
    off_b = blk_xor == 1
    off_c = blk_xor >= 2
    gc_col = jnp.dot(incl.astype(F32), g_d, preferred_element_type=F32, precision=HI)
    gc_row = jnp.dot(gt_d, (order <= 0).astype(F32), preferred_element_type=F32, precision=HI)
    gl_row = jnp.sum(g_d, axis=0, keepdims=True)
    yield

    hs = range(nh)
    sls = [slice(h * HEAD_DIM, (h + 1) * HEAD_DIM) for h in hs]
    gcc = [gc_col[:, h:h + 1] for h in hs]
    bc = [beta_d[:, h:h + 1] for h in hs]
    gl = [gl_row[:, h:h + 1] for h in hs]
    decay = [jnp.where(incl, jnp.exp(gcc[h] - gc_row[h:h + 1, :]), 0.0) for h in hs]
    kb = [k_ref[:, sls[h]] * bc[h] for h in hs]
    kkqk = [_dot_nt(jnp.concatenate([kb[h], q_ref[:, sls[h]]], axis=0).astype(BF16),
                    k_ref[:, sls[h]].astype(BF16)) for h in hs]
    yield
    lmat = [jnp.where(strict, kkqk[h][0:CHUNK] * decay[h], 0.0) for h in hs]
    a_intra = [(kkqk[h][CHUNK:] * decay[h]).astype(BF16) for h in hs]
    p16 = [(-jnp.where(blk_a, lmat[h], 0.0)).astype(BF16) for h in hs]
    inv = [eye - jnp.where(blk_a, lmat[h], 0.0) for h in hs]
    for _ in range(INV_BLOCK.bit_length() - 2):
        p16 = [_dot(p16[h], p16[h]).astype(BF16) for h in hs]
        yield
        inv = [inv[h] + _dot(inv[h].astype(BF16), p16[h]) for h in hs]
        yield
    for off in (off_b, off_c):
        inv16 = [inv[h].astype(BF16) for h in hs]
        t = [_dot(jnp.where(off, lmat[h], 0.0).astype(BF16), inv16[h]).astype(BF16) for h in hs]
        yield
        inv = [inv[h] - _dot(inv16[h], t[h]) for h in hs]
        yield
    eg = [jnp.exp(gcc[h]) for h in hs]
    uw = [_dot(inv[h].astype(BF16),
               jnp.concatenate([v_ref[:, sls[h]] * bc[h], kb[h] * eg[h]], axis=1).astype(BF16)) for h in hs]
    yield
    z_state = jnp.zeros((HEAD_DIM, HEAD_DIM), BF16)
    z_val = jnp.zeros((CHUNK, HEAD_DIM), BF16)
    pairs = [(h, h + 1) for h in range(0, nh, 2)]
    wq = []
    for a, b in pairs:
        lhs = jnp.concatenate([jnp.concatenate([uw[x][:, HEAD_DIM:], q_ref[:, sls[x]] * eg[x]], axis=0)
                               for x in (a, b)], axis=1).astype(BF16)
        s_bd = jnp.concatenate([jnp.concatenate([st_ref[a].astype(BF16), z_state], axis=1),
                                jnp.concatenate([z_state, st_ref[b].astype(BF16)], axis=1)], axis=0)
        wq.append(_dot(lhs, s_bd))
    yield
    v16 = []
    for i, (a, b) in enumerate(pairs):
        v16.append((uw[a][:, 0:HEAD_DIM] - wq[i][0:CHUNK, 0:HEAD_DIM]).astype(BF16))
        v16.append((uw[b][:, 0:HEAD_DIM] - wq[i][0:CHUNK, HEAD_DIM:]).astype(BF16))
    for i, (a, b) in enumerate(pairs):
        o_ref[:, sls[a]] = wq[i][CHUNK:, 0:HEAD_DIM] + _dot(a_intra[a], v16[a])
        o_ref[:, sls[b]] = wq[i][CHUNK:, HEAD_DIM:] + _dot(a_intra[b], v16[b])
    yield
    for a, b in pairs:
        k_dec = jnp.concatenate([k_ref[:, sls[x]] * jnp.exp(gl[x] - gcc[x]) for x in (a, b)], axis=0)
        v_bd = jnp.concatenate([jnp.concatenate([v16[a], z_val], axis=1),
                                jnp.concatenate([z_val, v16[b]], axis=1)], axis=0)
        upd = _dot(k_dec.T.astype(BF16), v_bd)
        st_ref[a] = st_ref[a] * jnp.exp(gl[a]) + upd[:, 0:HEAD_DIM]
        st_ref[b] = st_ref[b] * jnp.exp(gl[b]) + upd[:, HEAD_DIM:]


def _gdn_scan(qkvn, small, abt3, a_log, dt_bias, rows, ab_block):
    r = qkvn.shape[0]
    ncc, ncl = rows.c // CHUNK, rows.l // CHUNK
    lat_blocks = rows.t // CHUNK

    def rb(fwd, b, s):
        cc = s if fwd else ncc - 1 - s
        lc = s - ncc if fwd else ncc + ncl - 1 - s
        return jnp.where(s < ncc, lat_blocks + b * ncc + cc, b * ncl + lc)

    def dir_specs(fwd):
        return [pl.BlockSpec((CHUNK, w), lambda b, s: (rb(fwd, b, s), 0)),
                pl.BlockSpec((CHUNK, w), lambda b, s: (rb(fwd, b, s), 1)),
                pl.BlockSpec((CHUNK, w), lambda b, s: (rb(fwd, b, s), 2)),
                pl.BlockSpec((CHUNK, LANES), lambda b, s: (rb(fwd, b, s), ab_block)),
                pl.BlockSpec((None, CHUNK, CHUNK), lambda b, s: (rb(fwd, b, s), 0, 0))]

    w = QK_W
    al = a_log.reshape(1, 2 * HEADS)
    dtb = dt_bias.reshape(1, 2 * HEADS)
    vec_spec = pl.BlockSpec((1, 2 * HEADS), lambda b, s: (0, 0))
    col_spec = pl.BlockSpec((2 * HEADS, 1), lambda b, s: (0, 0))
    return pl.pallas_call(
        _gdn_scan_kernel,
        grid=(rows.b, ncc + ncl),
        in_specs=dir_specs(True) + dir_specs(False) + [vec_spec, vec_spec, col_spec, col_spec],
        out_specs=(pl.BlockSpec((CHUNK, w), lambda b, s: (rb(True, b, s), 0)),
                   pl.BlockSpec((CHUNK, w), lambda b, s: (rb(False, b, s), 0))),
        out_shape=(jax.ShapeDtypeStruct((r, w), F32), jax.ShapeDtypeStruct((r, w), F32)),
        scratch_shapes=[pltpu.VMEM((2, HEADS, HEAD_DIM, HEAD_DIM), F32)],
        compiler_params=_cparams(("parallel", "arbitrary"), 40),
        name="gdn_scan",
    )(qkvn, qkvn, qkvn, small, abt3, qkvn, qkvn, qkvn, small, abt3, al, dtb, al.reshape(-1, 1), dtb.reshape(-1, 1))


def _gdn_out_kernel(of_ref, ob_ref, z_ref, g_ref, y_ref):
    o = of_ref[...] + ob_ref[...]
    z = z_ref[...]
    for h in range(HEADS):
        sl = slice(h * HEAD_DIM, (h + 1) * HEAD_DIM)
        seg = o[:, sl]
        ms = jnp.mean(seg * seg, -1, keepdims=True)
        y_ref[:, sl] = (seg * lax.rsqrt(ms + EPS) * g_ref[...] * _silu(z[:, sl])).astype(y_ref.dtype)


def _gdn_out(o_fwd, o_bwd, z, gdn_norm, m):
    w = QK_W
    return pl.pallas_call(
        _gdn_out_kernel,
        grid=(m // ROW_TILE,),
        in_specs=[pl.BlockSpec((ROW_TILE, w), lambda i: (i, 0)),
                  pl.BlockSpec((ROW_TILE, w), lambda i: (i, 0)),
                  pl.BlockSpec((ROW_TILE, w), lambda i: (i, 0)),
                  pl.BlockSpec((1, HEAD_DIM), lambda i: (0, 0))],
        out_specs=pl.BlockSpec((ROW_TILE, w), lambda i: (i, 0)),
        out_shape=jax.ShapeDtypeStruct((m, w), BF16),
        compiler_params=_cparams(("parallel",)),
        name="gdn_out",
    )(o_fwd, o_bwd, z, gdn_norm.reshape(1, HEAD_DIM))


ATTN_SUB = 256
Q_HEAD_W = 2 * HEAD_DIM
UQ_HEAD_COLS = 3 * HEAD_DIM


def _rms(x, g):
    return x * lax.rsqrt(jnp.mean(x * x, -1, keepdims=True) + EPS) * g


def _mla_q_kernel(x_ref, g_ref, w_ref, tab_ref, o_ref):
    xn = _rms(x_ref[...], g_ref[...]).astype(BF16)
    cos_s = tab_ref[:, 0:LANES]
    sin_s = tab_ref[:, LANES:2 * LANES]
    for h in range(HEADS):
        r = _dot(xn, w_ref[:, h * UQ_HEAD_COLS:(h + 1) * UQ_HEAD_COLS])
        o_ref[:, h * Q_HEAD_W:h * Q_HEAD_W + LANES] = (r[:, 0:LANES] * Q_SCALE).astype(o_ref.dtype)
        rope = r[:, LANES:2 * LANES] * cos_s + r[:, 2 * LANES:] * sin_s
        o_ref[:, h * Q_HEAD_W + LANES:(h + 1) * Q_HEAD_W] = rope.astype(o_ref.dtype)


def _mla_q(small, q_norm, w_ext, tab, m, tm):
    return pl.pallas_call(
        _mla_q_kernel,
        grid=(m // tm,),
        in_specs=[pl.BlockSpec((tm, Q_LORA), lambda i: (i, 0)),
                  pl.BlockSpec((1, Q_LORA), lambda i: (0, 0)),
                  pl.BlockSpec(w_ext.shape, lambda i: (0, 0)),
                  pl.BlockSpec((tm, 4 * LANES), lambda i: (i, 0))],
        out_specs=pl.BlockSpec((tm, HEADS * Q_HEAD_W), lambda i: (i, 0)),
        out_shape=jax.ShapeDtypeStruct((m, HEADS * Q_HEAD_W), BF16),
        compiler_params=_cparams(("parallel",), 48),
        name="mla_q",
    )(small, q_norm.reshape(1, Q_LORA), w_ext, tab)


def _mla_kv_kernel(x_ref, kr_ref, g_ref, w_ref, tab_ref, k_ref, v_ref):
    xn = _rms(x_ref[...], g_ref[...]).astype(BF16)
    kr = kr_ref[...]
    rope = kr[:, 0:LANES] * tab_ref[:, 2 * LANES:3 * LANES] + kr[:, LANES:] * tab_ref[:, 3 * LANES:]
    rope = rope.astype(k_ref.dtype)
    for h in range(HEADS):
        r = _dot(xn, w_ref[:, h * Q_HEAD_W:(h + 1) * Q_HEAD_W])
        k_ref[:, h * Q_HEAD_W:h * Q_HEAD_W + LANES] = r[:, 0:LANES].astype(k_ref.dtype)
        k_ref[:, h * Q_HEAD_W + LANES:(h + 1) * Q_HEAD_W] = rope
        v_ref[:, h * HEAD_DIM:(h + 1) * HEAD_DIM] = r[:, LANES:].astype(v_ref.dtype)


def _mla_kv(small, kv_norm, w_ukv, tab, m, tm):
    return pl.pallas_call(
        _mla_kv_kernel,
        grid=(m // tm,),
        in_specs=[pl.BlockSpec((tm, KV_LORA), lambda i: (i, 1)),
                  pl.BlockSpec((tm, 2 * LANES), lambda i: (i, 4)),
                  pl.BlockSpec((1, KV_LORA), lambda i: (0, 0)),
                  pl.BlockSpec(w_ukv.shape, lambda i: (0, 0)),
                  pl.BlockSpec((tm, 4 * LANES), lambda i: (i, 0))],
        out_specs=(pl.BlockSpec((tm, HEADS * Q_HEAD_W), lambda i: (i, 0)),
                   pl.BlockSpec((tm, HEADS * HEAD_DIM), lambda i: (i, 0))),
        out_shape=(jax.ShapeDtypeStruct((m, HEADS * Q_HEAD_W), BF16),
                   jax.ShapeDtypeStruct((m, HEADS * HEAD_DIM), BF16)),
        compiler_params=_cparams(("parallel",), 48),
        name="mla_kv",
    )(small, small, kv_norm.reshape(1, KV_LORA), w_ukv, tab)


def _attn_kernel(*refs, with_latent):
    if with_latent:
        q_ref, kl_ref, vl_ref, kc_ref, vc_ref, o_ref = refs
    else:
        q_ref, kc_ref, vc_ref, o_ref = refs
    subs = [slice(i, i + ATTN_SUB) for i in range(0, q_ref.shape[0], ATTN_SUB)]

    def scores(s):
        q = q_ref[s, :]
        return (_dot_nt(q, kc_ref[...]), _dot_nt(q, kl_ref[...]) if with_latent else None)

    def finish(s, sc, sl):
        m = jnp.max(sc, -1, keepdims=True)
        if with_latent:
            m = jnp.maximum(m, jnp.max(sl, -1, keepdims=True))
        pc = jnp.exp2(sc - m)
        den = jnp.sum(pc, -1, keepdims=True)
        acc = _dot(pc.astype(BF16), vc_ref[...])
        if with_latent:
            p_lat = jnp.exp2(sl - m)
            den = den + jnp.sum(p_lat, -1, keepdims=True)
            acc = acc + _dot(p_lat.astype(BF16), vl_ref[...])
        o_ref[s, :] = (acc / den).astype(o_ref.dtype)

    pending = None
    for s in subs:
        nxt = (s,) + scores(s)
        if pending is not None:
            finish(*pending)
        pending = nxt
    finish(*pending)


def _attention(qf, kf, vf, rows, latent_queries, tq):
    b, l, c = rows.b, rows.l, rows.c
    ctx0 = rows.t // c
    kc_spec = pl.BlockSpec((c, Q_HEAD_W), lambda bi, h, qi: (ctx0 + bi, h))
    vc_spec = pl.BlockSpec((c, HEAD_DIM), lambda bi, h, qi: (ctx0 + bi, h))
    if latent_queries:
        nq = l // tq
        q_spec = pl.BlockSpec((tq, Q_HEAD_W), lambda bi, h, qi: (bi * nq + qi, h))
        o_spec = pl.BlockSpec((tq, HEAD_DIM), lambda bi, h, qi: (bi * nq + qi, h))
        in_specs = [q_spec,
                    pl.BlockSpec((l, Q_HEAD_W), lambda bi, h, qi: (bi, h)),
                    pl.BlockSpec((l, HEAD_DIM), lambda bi, h, qi: (bi, h)),
                    kc_spec, vc_spec]
        args = (qf, kf, vf, kf, vf)
        m_out = rows.t
    else:
        nq = c // tq
        q0 = rows.t // tq
        q_spec = pl.BlockSpec((tq, Q_HEAD_W), lambda bi, h, qi: (q0 + bi * nq + qi, h))
        o_spec = pl.BlockSpec((tq, HEAD_DIM), lambda bi, h, qi: (bi * nq + qi, h))
        in_specs = [q_spec, kc_spec, vc_spec]
        args = (qf, kf, vf)
        m_out = rows.r - rows.t
    return pl.pallas_call(
        functools.partial(_attn_kernel, with_latent=latent_queries),
        grid=(b, HEADS, nq),
        in_specs=in_specs, out_specs=o_spec,
        out_shape=jax.ShapeDtypeStruct((m_out, HEADS * HEAD_DIM), BF16),
        compiler_params=_cparams(("parallel", "parallel", "arbitrary"), 48),
        name="mla_attn",
    )(*args)


def _router_kernel(h_ref, w_ref, ri_ref, pos_ref, cnt_ref, carry_ref, *, n_experts):
    i = pl.program_id(0)

    @pl.when(i == 0)
    def _():
        carry_ref[...] = jnp.zeros_like(carry_ref)

    logits = jnp.dot(h_ref[...], w_ref[...], preferred_element_type=F32, precision=HI)
    lane = lax.broadcasted_iota(jnp.int32, logits.shape, 1).astype(F32)
    neg = -jnp.inf
    logits = jnp.where(lane < n_experts, logits, neg)
    m1 = jnp.max(logits, -1, keepdims=True)
    i1 = jnp.min(jnp.where(logits == m1, lane, float(LANES)), -1, keepdims=True)
    sel1 = lane == i1
    rest = jnp.where(sel1, neg, logits)
    m2 = jnp.max(rest, -1, keepdims=True)
    i2 = jnp.min(jnp.where(rest == m2, lane, float(LANES)), -1, keepdims=True)
    sel2 = lane == i2
    e = jnp.exp(m2 - m1)
    w1 = 1.0 / (1.0 + e)
    w2 = e / (1.0 + e)
    ri_ref[...] = jnp.where(lane == 0, i1,
                            jnp.where(lane == 1, i2,
                                      jnp.where(lane == 2, w1, jnp.where(lane == 3, w2, 0.0))))
    onehot = jnp.where(sel1, 1.0, jnp.where(sel2, 1.0, 0.0))
    tile = logits.shape[0]
    ri_t = lax.broadcasted_iota(jnp.int32, (tile, tile), 0)
    ci_t = lax.broadcasted_iota(jnp.int32, (tile, tile), 1)
    before = (ci_t < ri_t).astype(BF16)
    pos_ref[...] = _dot(before, onehot.astype(BF16)) + carry_ref[0:1, :]
    carry_ref[0:1, :] = carry_ref[0:1, :] + jnp.sum(onehot, axis=0, keepdims=True)
    cnt_ref[...] = jnp.broadcast_to(carry_ref[0:1, :], cnt_ref.shape)


def _router(hf, router_w, n_experts):
    t, d = hf.shape
    wpad = jnp.zeros((d, LANES), F32).at[:, :n_experts].set(router_w)
    return pl.pallas_call(
        functools.partial(_router_kernel, n_experts=n_experts),
        grid=(t // ROW_TILE,),
        in_specs=[pl.BlockSpec((ROW_TILE, d), lambda i: (i, 0)),
                  pl.BlockSpec((d, LANES), lambda i: (0, 0))],
        out_specs=(pl.BlockSpec((ROW_TILE, LANES), lambda i: (i, 0)),
                   pl.BlockSpec((ROW_TILE, LANES), lambda i: (i, 0)),
                   pl.BlockSpec((8, LANES), lambda i: (0, 0))),
        out_shape=(jax.ShapeDtypeStruct((t, LANES), F32),
                   jax.ShapeDtypeStruct((t, LANES), F32),
                   jax.ShapeDtypeStruct((8, LANES), F32)),
        scratch_shapes=[pltpu.VMEM((8, LANES), F32)],
        compiler_params=_cparams(("arbitrary",)),
        name="moe_router",
    )(hf, wpad)


def _row_copy(src_hbm, row, buf, r, sem):
    return pltpu.make_async_copy(src_hbm.at[pl.ds(row, 1), :], buf.at[pl.ds(r, 1), :], sem)


GATHER_UNROLL = 8


def _gather_rows(idx_ref, base, n, src_hbm, buf, sem):
    def issue(it, c):
        for u in range(GATHER_UNROLL):
            r = it * GATHER_UNROLL + u
            _row_copy(src_hbm, idx_ref[base + r], buf, r, sem).start(priority=u % 2)
        return c

    lax.fori_loop(0, n // GATHER_UNROLL, issue, 0)
    pltpu.make_async_copy(src_hbm.at[pl.ds(0, n), :], buf, sem).wait()


def _moe_gather_kernel(valid_ref, tok_ref, h_hbm, o_ref, buf, sem):
    i = pl.program_id(0)

    @pl.when(valid_ref[i] > 0)
    def _():
        _gather_rows(tok_ref, i * MOE_TILE, MOE_TILE, h_hbm, buf, sem)
        o_ref[...] = buf[...].astype(o_ref.dtype)

    @pl.when(valid_ref[i] == 0)
    def _():
        o_ref[...] = jnp.zeros_like(o_ref)


def _moe_gather(valid, tok, hf, n_slots):
    d = hf.shape[1]
    return pl.pallas_call(
        _moe_gather_kernel,
        grid_spec=pltpu.PrefetchScalarGridSpec(
            num_scalar_prefetch=2,
            grid=(n_slots // MOE_TILE,),
            in_specs=[pl.BlockSpec(memory_space=pl.ANY)],
            out_specs=pl.BlockSpec((MOE_TILE, d), lambda i, v, t: (i, 0)),
            scratch_shapes=[pltpu.VMEM((MOE_TILE, d), F32), pltpu.SemaphoreType.DMA]),
        out_shape=jax.ShapeDtypeStruct((n_slots, d), BF16),
        compiler_params=_cparams(("arbitrary",)),
        name="moe_gather",
    )(valid, tok, hf)


def _moe_up_kernel(te_ref, valid_ref, x_ref, w1_ref, w3_ref, o_ref, w1b_ref, w3b_ref):
    i = pl.program_id(1)

    @pl.when((i == 0) | (te_ref[i] != te_ref[jnp.maximum(i - 1, 0)]))
    def _():
        w1b_ref[...] = w1_ref[...].astype(BF16)
        w3b_ref[...] = w3_ref[...].astype(BF16)

    @pl.when(valid_ref[i] > 0)
    def _():
        x = x_ref[...]
        o_ref[...] = (_silu(_dot(x, w1b_ref[...])) * _dot(x, w3b_ref[...])).astype(o_ref.dtype)

    @pl.when(valid_ref[i] == 0)
    def _():
        o_ref[...] = jnp.zeros_like(o_ref)


def _moe_up(te, valid, xs, w1, w3, layer, tf):
    s, d = xs.shape
    f = w1.shape[3]
    w_spec = pl.BlockSpec((None, None, d, tf), lambda j, i, te, v: (layer, te[i], 0, j))
    return pl.pallas_call(
        _moe_up_kernel,
        grid_spec=pltpu.PrefetchScalarGridSpec(
            num_scalar_prefetch=2,
            grid=(f // tf, s // MOE_TILE),
            in_specs=[pl.BlockSpec((MOE_TILE, d), lambda j, i, te, v: (i, 0)), w_spec, w_spec],
            out_specs=pl.BlockSpec((MOE_TILE, tf), lambda j, i, te, v: (i, j)),
            scratch_shapes=[pltpu.VMEM((d, tf), BF16), pltpu.VMEM((d, tf), BF16)]),
        out_shape=jax.ShapeDtypeStruct((s, f), BF16),
        compiler_params=_cparams(("arbitrary", "arbitrary"), 48),
        name="moe_up",
    )(te, valid, xs, w1, w3)


def _moe_down_kernel(te_ref, valid_ref, x_ref, w_ref, o_ref):
    i = pl.program_id(1)

    @pl.when(valid_ref[i] > 0)
    def _():
        o_ref[...] = _dot(x_ref[...], w_ref[...])

    @pl.when(valid_ref[i] == 0)
    def _():
        o_ref[...] = jnp.zeros_like(o_ref)


def _moe_down(te, valid, hmid, w2, tn):
    s, f = hmid.shape
    d = w2.shape[2]
    return pl.pallas_call(
        _moe_down_kernel,
        grid_spec=pltpu.PrefetchScalarGridSpec(
            num_scalar_prefetch=2,
            grid=(d // tn, s // MOE_TILE),
            in_specs=[pl.BlockSpec((MOE_TILE, f), lambda j, i, te, v: (i, 0)),
                      pl.BlockSpec((None, f, tn), lambda j, i, te, v: (te[i], 0, j))],
            out_specs=pl.BlockSpec((MOE_TILE, tn), lambda j, i, te, v: (i, j))),
        out_shape=jax.ShapeDtypeStruct((s, d), F32),
        compiler_params=_cparams(("arbitrary", "arbitrary"), 48),
        name="moe_down",
    )(te, valid, hmid, w2)


def _moe_combine_kernel(s1_ref, s2_ref, y_hbm, ri_ref, x_ref, gt_ref, g_ref, b_ref, o_ref,
                        buf1, buf2, sem1, sem2, *, alpha):
    i = pl.program_id(0)
    base = i * ROW_TILE

    def issue(it, c):
        for u in range(GATHER_UNROLL):
            r = it * GATHER_UNROLL + u
            _row_copy(y_hbm, s1_ref[base + r], buf1, r, sem1).start(priority=0)
            _row_copy(y_hbm, s2_ref[base + r], buf2, r, sem2).start(priority=1)
        return c

    lax.fori_loop(0, ROW_TILE // GATHER_UNROLL, issue, 0)
    pltpu.make_async_copy(y_hbm.at[pl.ds(0, ROW_TILE), :], buf1, sem1).wait()
    pltpu.make_async_copy(y_hbm.at[pl.ds(0, ROW_TILE), :], buf2, sem2).wait()
    ri = ri_ref[...]
    y = ri[:, 2:3] * buf1[...] + ri[:, 3:4] * buf2[...]
    xn = _layer_norm(alpha * x_ref[...] + gt_ref[...] * y)
    o_ref[...] = xn * g_ref[...] + b_ref[...]


def _moe_combine(slot1, slot2, yslot, ri, x, mods, gate_piece, ln_g, ln_b, rows, alpha):
    t, d = x.shape
    row_spec = pl.BlockSpec((ROW_TILE, d), lambda i, a, b: (i, 0))
    vec_spec = pl.BlockSpec((1, d), lambda i, a, b: (0, 0))
    return pl.pallas_call(
        functools.partial(_moe_combine_kernel, alpha=alpha),
        grid_spec=pltpu.PrefetchScalarGridSpec(
            num_scalar_prefetch=2,
            grid=(t // ROW_TILE,),
            in_specs=[pl.BlockSpec(memory_space=pl.ANY),
                      pl.BlockSpec((ROW_TILE, LANES), lambda i, a, b: (i, 0)),
                      row_spec,
                      pl.BlockSpec((None, None, 1, d), lambda i, a, b: (gate_piece, rows.group(i), 0, 0)),
                      vec_spec, vec_spec],
            out_specs=row_spec,
            scratch_shapes=[pltpu.VMEM((ROW_TILE, d), F32), pltpu.VMEM((ROW_TILE, d), F32),
                            pltpu.SemaphoreType.DMA, pltpu.SemaphoreType.DMA]),
        out_shape=jax.ShapeDtypeStruct((t, d), F32),
        compiler_params=_cparams(("arbitrary",)),
        name="moe_combine",
    )(slot1, slot2, yslot, ri, x, mods, ln_g, ln_b)


def _moe_plan(ri, pos, cnt, n_experts, n_slots):
    t = ri.shape[0]
    i1 = ri[:, 0].astype(jnp.int32)
    i2 = ri[:, 1].astype(jnp.int32)
    pos8 = pos[:, :n_experts].astype(jnp.int32)
    counts = cnt[0, :n_experts].astype(jnp.int32)
    padded = ((counts + MOE_TILE - 1) // MOE_TILE) * MOE_TILE
    ends = jnp.cumsum(padded)
    offs = ends - padded
    slot1 = offs[i1] + jnp.take_along_axis(pos8, i1[:, None], axis=1)[:, 0]
    slot2 = offs[i2] + jnp.take_along_axis(pos8, i2[:, None], axis=1)[:, 0]
    ids = jnp.arange(t, dtype=jnp.int32)
    tok = jnp.zeros((n_slots,), jnp.int32).at[slot1].set(ids).at[slot2].set(ids)
    starts = jnp.arange(n_slots // MOE_TILE, dtype=jnp.int32) * MOE_TILE
    valid = (starts < ends[-1]).astype(jnp.int32)
    te = jnp.minimum(jnp.searchsorted(ends, starts, side="right").astype(jnp.int32), n_experts - 1)
    te = jnp.where(valid > 0, te, jnp.max(jnp.where(valid > 0, te, 0)))
    return slot1, slot2, tok, te, valid


def _rope_tables(rows):
    l = rows.l
    nrow = l // GRID_W
    row = jnp.repeat(jnp.arange(nrow), GRID_W).astype(F32)
    col = jnp.tile(jnp.arange(GRID_W), nrow).astype(F32)
    n_freq = ROPE_DIM // 4
    inv_freq = ROPE_THETA ** (-jnp.arange(n_freq, dtype=F32) / n_freq)
    cr, sr = jnp.cos(row[:, None] * inv_freq), jnp.sin(row[:, None] * inv_freq)
    cc, sc = jnp.cos(col[:, None] * inv_freq), jnp.sin(col[:, None] * inv_freq)
    cos = jnp.concatenate([cr, cr, cc, cc], -1)
    sin = jnp.concatenate([-sr, sr, -sc, sc], -1)
    nctx = rows.r - rows.t
    cos = jnp.concatenate([jnp.tile(cos, (rows.b, 1)), jnp.ones((nctx, ROPE_DIM), F32)], 0)
    sin = jnp.concatenate([jnp.tile(sin, (rows.b, 1)), jnp.zeros((nctx, ROPE_DIM), F32)], 0)
    z = jnp.zeros((rows.r, LANES - ROPE_DIM), F32)
    return jnp.concatenate([Q_SCALE * cos, z, Q_SCALE * sin, z, cos, z, sin, z], -1)


_ROPE_SWAP = np.concatenate([np.arange(16, 32), np.arange(0, 16), np.arange(48, 64), np.arange(32, 48)])


def _uq_ext(w_uq):
    k = w_uq.shape[0]
    w = w_uq.reshape(k, HEADS, HEAD_DIM + ROPE_DIM)
    nope, rope = w[:, :, :HEAD_DIM], w[:, :, HEAD_DIM:]
    z = jnp.zeros((k, HEADS, LANES - ROPE_DIM), w.dtype)
    ext = jnp.concatenate([nope, rope, z, rope[:, :, _ROPE_SWAP], z], -1)
    return ext.reshape(k, HEADS * UQ_HEAD_COLS).astype(BF16)


def kernel(x, c, ctx, c_ctx, w_mod, b_mod, w_in, conv_w, a_log, dt_bias, gdn_norm, q_norm, kv_norm, w_uq, w_ukv, w_br_a, w_br_b, w_out, ln1_g, ln1_b, ln2_g, ln2_b, ffn_w1, ffn_w3, ffn_w2, moe_router, moe_w1, moe_w3, moe_w2):
    bsz, n_lat, d = x.shape
    n_ctx = ctx.shape[1]
    depth = w_mod.shape[0]
    assert depth == 2 and ffn_w1.shape[0] == 1 and moe_w1.shape[0] == 1
    rows = _Rows(bsz, n_lat, n_ctx)
    r_all, t_lat = rows.r, rows.t
    alpha = (2 * depth) ** 0.25
    n_experts = moe_router.shape[2]

    xs = jnp.concatenate([x.reshape(t_lat, d), ctx.reshape(r_all - t_lat, d)], 0)
    n_groups = -(-(bsz + 1) // 8) * 8
    cs = jnp.zeros((n_groups, d), F32).at[:bsz].set(c).at[bsz].set(c_ctx)
    tab = _rope_tables(rows)

    o_z = 3 * QK_W
    o_a = o_z + QK_W
    o_b = o_a + 2 * HEADS
    o_dq = o_b + 2 * HEADS
    o_dkv = o_dq + Q_LORA
    o_kr = o_dkv + KV_LORA
    o_ga = o_kr + ROPE_DIM
    ab_block = (Q_LORA + KV_LORA + 2 * LANES) // LANES

    tm_all = _pick(r_all, (1024, 512, 256))
    tm_lat = _pick(t_lat, (1024, 512, 256))
    tm_half = _pick(r_all, (512, 256))

    all_mods = [_mod_params(cs, w_mod, b_mod, i).reshape(n_groups, 6, d).transpose(1, 0, 2)[:, :, None, :]
                for i in range(depth)]

    h = None
    for i in range(depth):
        last = i == depth - 1
        m_mix = t_lat if last else r_all
        tm_mix = tm_lat if last else tm_all
        mods = all_mods[i]
        if i == 0:
            h = _ln_mod(xs, mods, rows, r_all, 0, 1, BF16)

        wi = w_in[i]
        kr_w = wi[:, o_kr:o_ga]
        zc = jnp.zeros((d, LANES - ROPE_DIM), F32)
        w_small = jnp.concatenate(
            [wi[:, o_dq:o_kr], kr_w, zc, kr_w[:, _ROPE_SWAP], zc, wi[:, o_a:o_dq], zc], 1).astype(BF16)
        qkv = _mm(h, wi[:, :o_z].astype(BF16), r_all, F32, tm_all, _pick(o_z, (1536, 1024, 512)))
        small = _mm(h, w_small, r_all, F32, tm_all, w_small.shape[1])
        z = _mm(h, wi[:, o_z:o_a].astype(BF16), m_mix, F32, tm_mix, 1024)
        gates = _mm(h, wi[:, o_ga:].astype(BF16), m_mix, F32, tm_mix, 1024)

        qkvn = _gdn_prep(qkv, conv_w[i], rows)
        ab = small[:, ab_block * LANES:ab_block * LANES + 4 * HEADS]
        abt3 = ab.reshape(r_all // CHUNK, CHUNK, 4 * HEADS).transpose(0, 2, 1)
        o_fwd, o_bwd = _gdn_scan(qkvn, small, abt3, a_log[i], dt_bias[i], rows, ab_block)
        ya = _gdn_out(o_fwd, o_bwd, z, gdn_norm[i], m_mix)

        qf = _mla_q(small, q_norm[i], _uq_ext(w_uq[i]), tab, m_mix, _pick(m_mix, (512, 256)))
        kf, vf = _mla_kv(small, kv_norm[i], w_ukv[i].astype(BF16), tab, r_all, tm_half)
        yb = _attention(qf, kf, vf, rows, True, _pick(n_lat, (512, 256)))
        if not last:
            yb_ctx = _attention(qf, kf, vf, rows, False, _pick(n_ctx, (256,)))
            yb = jnp.concatenate([yb, yb_ctx], 0)

        ym = _merge(ya, yb, w_br_a[i].astype(BF16), w_br_b[i].astype(BF16), gates, m_mix,
                    _pick(m_mix, (512, 256)), 1024)
        mres = _mm(ym, w_out[i].astype(BF16), m_mix, F32, tm_mix, 1024)
        moe_layer = i % 2 == 1
        x1, h2 = _resid_ln(xs, mres, mods, 2, ln1_g[i].reshape(1, d), ln1_b[i].reshape(1, d), rows, m_mix,
                           alpha, next_mods=(mods, 3, 4), next_dtype=F32 if moe_layer else BF16)

        g2, b2 = ln2_g[i].reshape(1, d), ln2_b[i].reshape(1, d)
        if not moe_layer:
            j = i // 2
            hmid = _swiglu_up(h2, ffn_w1[j].astype(BF16), ffn_w3[j].astype(BF16), m_mix, tm_mix, 512)
            f_out = _mm(hmid, ffn_w2[j].astype(BF16), m_mix, F32, _pick(m_mix, (512, 256)), 1024, n_outer=True)
            if last:
                xs = _resid_ln(x1, f_out, mods, 5, g2, b2, rows, m_mix, alpha)
            else:
                xs, h = _resid_ln(x1, f_out, mods, 5, g2, b2, rows, m_mix, alpha,
                                  next_mods=(all_mods[i + 1], 0, 1))
        else:
            assert last
            j = i // 2
            ri, pos, cnt = _router(h2, moe_router[j], n_experts)
            n_slots = TOP_K * t_lat + n_experts * MOE_TILE
            slot1, slot2, tok, te, valid = _moe_plan(ri, pos, cnt, n_experts, n_slots)
            xg = _moe_gather(valid, tok, h2, n_slots)
            hmid = _moe_up(te, valid, xg, moe_w1, moe_w3, j, 512)
            yslot = _moe_down(te, valid, hmid, moe_w2[j].astype(BF16), 512)
            xs = _moe_combine(slot1, slot2, yslot, ri, x1, mods, 5, g2, b2, rows, alpha)
    return xs[:t_lat].reshape(bsz, n_lat, d)
```

```python
import functools
import math

import numpy as np
import jax
import jax.numpy as jnp
from jax import lax
from jax.experimental import pallas as pl
from jax.experimental.pallas import tpu as pltpu

F32 = jnp.float32
BF16 = jnp.bfloat16
HI = lax.Precision.HIGHEST

HEADS = 16
HEAD_DIM = 128
ROPE_DIM = 64
Q_LORA = 512
KV_LORA = 512
CHUNK = 64
CONV_K = 5
GRID_W = 64
ROPE_THETA = 10000.0
QK_W = HEADS * HEAD_DIM
MLA_SCALE = (HEAD_DIM + ROPE_DIM) ** -0.5
Q_SCALE = MLA_SCALE * math.log2(math.e)
TOP_K = 2
EPS = 1e-6
ROW_TILE = 256
MOE_TILE = 512
LANES = 128


def _cparams(sem, vmem_mb=None):
    kw = dict(dimension_semantics=sem)
    if vmem_mb is not None:
        kw["vmem_limit_bytes"] = vmem_mb << 20
    return pltpu.CompilerParams(**kw)


def _pick(m, cands):
    for c in cands:
        if m % c == 0:
            return c
    raise ValueError(f"no tile for {m}")


def _dot(a, b):
    return jnp.dot(a, b, preferred_element_type=F32)


def _dot_nt(a, b):
    return lax.dot_general(a, b, (((1,), (1,)), ((), ())), preferred_element_type=F32)


def _silu(x):
    return x * jax.nn.sigmoid(x)


def _mod_kernel(c_ref, w_ref, b_ref, o_ref):
    s = _silu(c_ref[...])
    o_ref[...] = jnp.dot(s, w_ref[...], preferred_element_type=F32, precision=HI) + b_ref[...]


def _mod_params(cs, w_mod, b_mod, layer):
    g, d = cs.shape
    n = w_mod.shape[2]
    tn = _pick(n, (1024, 512))
    return pl.pallas_call(
        _mod_kernel,
        grid=(n // tn,),
        in_specs=[pl.BlockSpec((g, d), lambda j: (0, 0)),
                  pl.BlockSpec((None, d, tn), lambda j: (layer, 0, j)),
                  pl.BlockSpec((None, 1, tn), lambda j: (layer, 0, j))],
        out_specs=pl.BlockSpec((g, tn), lambda j: (0, j)),
        out_shape=jax.ShapeDtypeStruct((g, n), F32),
        compiler_params=_cparams(("arbitrary",), 40),
        name="mod_params",
    )(cs, w_mod, b_mod.reshape(b_mod.shape[0], 1, n))


def _layer_norm(x):
    mu = jnp.mean(x, -1, keepdims=True)
    xc = x - mu
    var = jnp.mean(xc * xc, -1, keepdims=True)
    return xc * lax.rsqrt(var + EPS)


def _ln_mod_kernel(x_ref, sh_ref, sc_ref, o_ref):
    y = _layer_norm(x_ref[...])
    o_ref[...] = (y * (1.0 + sc_ref[...]) + sh_ref[...]).astype(o_ref.dtype)


def _resid_ln_kernel(x_ref, m_ref, gt_ref, g_ref, b_ref, *rest, alpha, with_mod):
    y = _layer_norm(alpha * x_ref[...] + gt_ref[...] * m_ref[...])
    xn = y * g_ref[...] + b_ref[...]
    if with_mod:
        sh_ref, sc_ref, xo_ref, ho_ref = rest
        xo_ref[...] = xn
        ho_ref[...] = (_layer_norm(xn) * (1.0 + sc_ref[...]) + sh_ref[...]).astype(ho_ref.dtype)
    else:
        (xo_ref,) = rest
        xo_ref[...] = xn


def _group_of_tile(i, lat_tiles, tiles_per_seq, n_batch):
    return jnp.where(i < lat_tiles, i // tiles_per_seq, n_batch)


class _Rows:
    def __init__(self, n_batch, n_lat, n_ctx):
        self.b, self.l, self.c = n_batch, n_lat, n_ctx
        self.t = n_batch * n_lat
        self.r = n_batch * (n_lat + n_ctx)
        self.lat_tiles = self.t // ROW_TILE
        self.lat_tps = n_lat // ROW_TILE
        self.ctx_tps = n_ctx // ROW_TILE

    def group(self, i):
        return _group_of_tile(i, self.lat_tiles, self.lat_tps, self.b)


def _mod_spec(rows, piece, d):
    return pl.BlockSpec((None, None, 1, d), lambda i: (piece, rows.group(i), 0, 0))


def _ln_mod(x, mods, rows, m, shift_piece, scale_piece, out_dtype):
    d = x.shape[1]
    return pl.pallas_call(
        _ln_mod_kernel,
        grid=(m // ROW_TILE,),
        in_specs=[pl.BlockSpec((ROW_TILE, d), lambda i: (i, 0)),
                  _mod_spec(rows, shift_piece, d), _mod_spec(rows, scale_piece, d)],
        out_specs=pl.BlockSpec((ROW_TILE, d), lambda i: (i, 0)),
        out_shape=jax.ShapeDtypeStruct((m, d), out_dtype),
        compiler_params=_cparams(("parallel",)),
        name="ln_mod",
    )(x, mods, mods)


def _resid_ln(x, mres, mods, gate_piece, ln_g, ln_b, rows, m, alpha, next_mods=None, next_dtype=BF16):
    d = x.shape[1]
    row_spec = pl.BlockSpec((ROW_TILE, d), lambda i: (i, 0))
    vec_spec = pl.BlockSpec((1, d), lambda i: (0, 0))
    in_specs = [row_spec, row_spec, _mod_spec(rows, gate_piece, d), vec_spec, vec_spec]
    args = [x, mres, mods, ln_g, ln_b]
    if next_mods is None:
        out_specs = row_spec
        out_shape = jax.ShapeDtypeStruct((m, d), F32)
    else:
        nm, sh_piece, sc_piece = next_mods
        in_specs += [_mod_spec(rows, sh_piece, d), _mod_spec(rows, sc_piece, d)]
        args += [nm, nm]
        out_specs = (row_spec, row_spec)
        out_shape = (jax.ShapeDtypeStruct((m, d), F32), jax.ShapeDtypeStruct((m, d), next_dtype))
    return pl.pallas_call(
        functools.partial(_resid_ln_kernel, alpha=alpha, with_mod=next_mods is not None),
        grid=(m // ROW_TILE,),
        in_specs=in_specs, out_specs=out_specs, out_shape=out_shape,
        compiler_params=_cparams(("parallel",)),
        name="resid_ln",
    )(*args)


def _mm_kernel(x_ref, w_ref, o_ref):
    o_ref[...] = _dot(x_ref[...], w_ref[...]).astype(o_ref.dtype)


def _mm(x, w, m, out_dtype, tm, tn, n_outer=False, vmem_mb=48):
    k, n = w.shape
    if n_outer:
        grid = (n // tn, m // tm)
        xi, wi, oi = (lambda j, i: (i, 0)), (lambda j, i: (0, j)), (lambda j, i: (i, j))
    else:
        grid = (m // tm, n // tn)
        xi, wi, oi = (lambda i, j: (i, 0)), (lambda i, j: (0, j)), (lambda i, j: (i, j))
    return pl.pallas_call(
        _mm_kernel,
        grid=grid,
        in_specs=[pl.BlockSpec((tm, k), xi), pl.BlockSpec((k, tn), wi)],
        out_specs=pl.BlockSpec((tm, tn), oi),
        out_shape=jax.ShapeDtypeStruct((m, n), out_dtype),
        compiler_params=_cparams(("parallel", "parallel"), vmem_mb),
        name="mm",
    )(x, w)


def _swiglu_up_kernel(x_ref, w1_ref, w3_ref, o_ref):
    x = x_ref[...]
    o_ref[...] = (_silu(_dot(x, w1_ref[...])) * _dot(x, w3_ref[...])).astype(o_ref.dtype)


def _swiglu_up(x, w1, w3, m, tm, tf):
    k, f = w1.shape
    return pl.pallas_call(
        _swiglu_up_kernel,
        grid=(m // tm, f // tf),
        in_specs=[pl.BlockSpec((tm, k), lambda i, j: (i, 0)),
                  pl.BlockSpec((k, tf), lambda i, j: (0, j)),
                  pl.BlockSpec((k, tf), lambda i, j: (0, j))],
        out_specs=pl.BlockSpec((tm, tf), lambda i, j: (i, j)),
        out_shape=jax.ShapeDtypeStruct((m, f), BF16),
        compiler_params=_cparams(("parallel", "parallel"), 48),
        name="swiglu_up",
    )(x, w1, w3)


def _merge_kernel(ya_ref, yb_ref, wa_ref, wb_ref, ga_ref, gb_ref, o_ref):
    a = _dot(ya_ref[...], wa_ref[...])
    b = _dot(yb_ref[...], wb_ref[...])
    o_ref[...] = (jax.nn.sigmoid(ga_ref[...]) * a + jax.nn.sigmoid(gb_ref[...]) * b).astype(o_ref.dtype)


def _merge(ya, yb, wa, wb, gates, m, tm, tn):
    k, n = wa.shape
    nb = n // tn
    return pl.pallas_call(
        _merge_kernel,
        grid=(nb, m // tm),
        in_specs=[pl.BlockSpec((tm, k), lambda j, i: (i, 0)),
                  pl.BlockSpec((tm, k), lambda j, i: (i, 0)),
                  pl.BlockSpec((k, tn), lambda j, i: (0, j)),
                  pl.BlockSpec((k, tn), lambda j, i: (0, j)),
                  pl.BlockSpec((tm, tn), lambda j, i: (i, j)),
                  pl.BlockSpec((tm, tn), lambda j, i: (i, nb + j))],
        out_specs=pl.BlockSpec((tm, tn), lambda j, i: (i, j)),
        out_shape=jax.ShapeDtypeStruct((m, n), BF16),
        compiler_params=_cparams(("parallel", "parallel"), 48),
        name="merge",
    )(ya, yb, wa, wb, gates, gates)


CONV_COLS = 2048
HALO = 8
INV_BLOCK = 16
assert CHUNK == 4 * INV_BLOCK


def _gdn_prep_kernel(x_ref, p_ref, n_ref, w_ref, o_ref, buf_ref, *, rows):
    i = pl.program_id(0)
    j = pl.program_id(1)
    is_lat = i < rows.lat_tiles
    pos = jnp.where(is_lat, i % rows.lat_tps, (i - rows.lat_tiles) % rows.ctx_tps)
    tps = jnp.where(is_lat, rows.lat_tps, rows.ctx_tps)
    buf_ref[0:HALO, :] = jnp.where(pos == 0, 0.0, p_ref[...])
    buf_ref[HALO:HALO + ROW_TILE, :] = x_ref[...]
    buf_ref[HALO + ROW_TILE:, :] = jnp.where(pos == tps - 1, 0.0, n_ref[...])
    base = HALO - CONV_K // 2
    acc = w_ref[0:1, :] * buf_ref[base:base + ROW_TILE, :]
    for t in range(1, CONV_K):
        acc = acc + w_ref[t:t + 1, :] * buf_ref[base + t:base + t + ROW_TILE, :]
    y = _silu(acc)
    kind = j // (QK_W // CONV_COLS)
    scale = jnp.where(kind == 0, HEAD_DIM ** -0.5, 1.0)
    for hh in range(CONV_COLS // HEAD_DIM):
        sl = slice(hh * HEAD_DIM, (hh + 1) * HEAD_DIM)
        seg = y[:, sl]
        ss = jnp.sum(seg * seg, -1, keepdims=True)
        nrm = seg * (lax.rsqrt(ss + EPS) * scale)
        o_ref[:, sl] = jnp.where(kind == 2, seg, nrm)


def _gdn_prep(qkv, conv_w, rows):
    r, n = qkv.shape
    hb = ROW_TILE // HALO
    last = r // HALO - 1
    return pl.pallas_call(
        functools.partial(_gdn_prep_kernel, rows=rows),
        grid=(r // ROW_TILE, n // CONV_COLS),
        in_specs=[pl.BlockSpec((ROW_TILE, CONV_COLS), lambda i, j: (i, j)),
                  pl.BlockSpec((HALO, CONV_COLS), lambda i, j: (jnp.maximum(i * hb - 1, 0), j)),
                  pl.BlockSpec((HALO, CONV_COLS), lambda i, j: (jnp.minimum((i + 1) * hb, last), j)),
                  pl.BlockSpec((CONV_K, CONV_COLS), lambda i, j: (0, j))],
        out_specs=pl.BlockSpec((ROW_TILE, CONV_COLS), lambda i, j: (i, j)),
        out_shape=jax.ShapeDtypeStruct((r, n), F32),
        scratch_shapes=[pltpu.VMEM((ROW_TILE + 2 * HALO, CONV_COLS), F32)],
        compiler_params=_cparams(("parallel", "parallel")),
        name="gdn_prep",
    )(qkv, qkv, qkv, conv_w)


def _softplus(x):
    return jnp.maximum(x, 0.0) + jnp.log(1.0 + jnp.exp(-jnp.abs(x)))


def _gdn_scan_kernel(qf_ref, kf_ref, vf_ref, abf_ref, abtf_ref, qb_ref, kb_ref, vb_ref, abb_ref, abtb_ref,
                     al_ref, dtb_ref, alt_ref, dtbt_ref, of_ref, ob_ref, st_ref):
    @pl.when(pl.program_id(1) == 0)
    def _():
        st_ref[...] = jnp.zeros_like(st_ref)

    shared = (al_ref, dtb_ref, alt_ref, dtbt_ref)
    pending = [_scan_direction(True, qf_ref, kf_ref, vf_ref, abf_ref, abtf_ref, *shared, of_ref, st_ref.at[0]),
               _scan_direction(False, qb_ref, kb_ref, vb_ref, abb_ref, abtb_ref, *shared, ob_ref, st_ref.at[1])]
    while pending:
        for gen in list(pending):
            try:
                next(gen)
            except StopIteration:
                pending.remove(gen)


def _scan_direction(fwd, q_ref, k_ref, v_ref, ab_ref, abt_ref, al_ref, dtb_ref, alt_ref, dtbt_ref, o_ref, st_ref):
    nh = HEADS
    dcols = slice(0, nh) if fwd else slice(nh, 2 * nh)
    ab = ab_ref[...]
    g_all = -jnp.exp(al_ref[...]) * _softplus(ab[:, 0:2 * nh] + dtb_ref[...])
    g_d = g_all[:, dcols]
    beta_d = jax.nn.sigmoid(ab[:, 2 * nh:4 * nh])[:, dcols]
    abt = abt_ref[...]
    gt_d = (-jnp.exp(alt_ref[...]) * _softplus(abt[0:2 * nh, :] + dtbt_ref[...]))[dcols, :]

    ri = lax.broadcasted_iota(jnp.int32, (CHUNK, CHUNK), 0)
    ci = lax.broadcasted_iota(jnp.int32, (CHUNK, CHUNK), 1)
    order = (ri - ci) if fwd else (ci - ri)
    incl = order >= 0
    strict = order > 0
    eye = (ri == ci).astype(F32)
    shift = INV_BLOCK.bit_length() - 1
    blk_xor = lax.shift_right_logical(ri, shift) ^ lax.shift_right_logical(ci, shift)
    blk_a = blk_xor == 0
    off_b = blk_xor == 1
    off_c = blk_xor >= 2
    gc_col = jnp.dot(incl.astype(F32), g_d, preferred_element_type=F32, precision=HI)
    gc_row = jnp.dot(gt_d, (order <= 0).astype(F32), preferred_element_type=F32, precision=HI)
    gl_row = jnp.sum(g_d, axis=0, keepdims=True)
    yield

    hs = range(nh)
    sls = [slice(h * HEAD_DIM, (h + 1) * HEAD_DIM) for h in hs]
    gcc = [gc_col[:, h:h + 1] for h in hs]
    bc = [beta_d[:, h:h + 1] for h in hs]
    gl = [gl_row[:, h:h + 1] for h in hs]
    decay = [jnp.where(incl, jnp.exp(gcc[h] - gc_row[h:h + 1, :]), 0.0) for h in hs]
    kb = [k_ref[:, sls[h]] * bc[h] for h in hs]
    kkqk = [_dot_nt(jnp.concatenate([kb[h], q_ref[:, sls[h]]], axis=0).astype(BF16),
                    k_ref[:, sls[h]].astype(BF16)) for h in hs]
    yield
    lmat = [jnp.where(strict, kkqk[h][0:CHUNK] * decay[h], 0.0) for h in hs]
    a_intra = [(kkqk[h][CHUNK:] * decay[h]).astype(BF16) for h in hs]
    p16 = [(-jnp.where(blk_a, lmat[h], 0.0)).astype(BF16) for h in hs]
    inv = [eye - jnp.where(blk_a, lmat[h], 0.0) for h in hs]
    for _ in range(INV_BLOCK.bit_length() - 2):
        p16 = [_dot(p16[h], p16[h]).astype(BF16) for h in hs]
        yield
        inv = [inv[h] + _dot(inv[h].astype(BF16), p16[h]) for h in hs]
        yield
    for off in (off_b, off_c):
        inv16 = [inv[h].astype(BF16) for h in hs]
        t = [_dot(jnp.where(off, lmat[h], 0.0).astype(BF16), inv16[h]).astype(BF16) for h in hs]
        yield
        inv = [inv[h] - _dot(inv16[h], t[h]) for h in hs]
        yield
    eg = [jnp.exp(gcc[h]) for h in hs]
    uw = [_dot(inv[h].astype(BF16),
               jnp.concatenate([v_ref[:, sls[h]] * bc[h], kb[h] * eg[h]], axis=1).astype(BF16)) for h in hs]
    yield
    z_state = jnp.zeros((HEAD_DIM, HEAD_DIM), BF16)
    z_val = jnp.zeros((CHUNK, HEAD_DIM), BF16)
    pairs = [(h, h + 1) for h in range(0, nh, 2)]
    wq = []
    for a, b in pairs:
        lhs = jnp.concatenate([jnp.concatenate([uw[x][:, HEAD_DIM:], q_ref[:, sls[x]] * eg[x]], axis=0)
                               for x in (a, b)], axis=1).astype(BF16)
        s_bd = jnp.concatenate([jnp.concatenate([st_ref[a].astype(BF16), z_state], axis=1),
                                jnp.concatenate([z_state, st_ref[b].astype(BF16)], axis=1)], axis=0)
        wq.append(_dot(lhs, s_bd))
    yield
    v16 = []
    for i, (a, b) in enumerate(pairs):
        v16.append((uw[a][:, 0:HEAD_DIM] - wq[i][0:CHUNK, 0:HEAD_DIM]).astype(BF16))
        v16.append((uw[b][:, 0:HEAD_DIM] - wq[i][0:CHUNK, HEAD_DIM:]).astype(BF16))
    for i, (a, b) in enumerate(pairs):
        o_ref[:, sls[a]] = wq[i][CHUNK:, 0:HEAD_DIM] + _dot(a_intra[a], v16[a])
        o_ref[:, sls[b]] = wq[i][CHUNK:, HEAD_DIM:] + _dot(a_intra[b], v16[b])
    yield
    for a, b in pairs:
        k_dec = jnp.concatenate([k_ref[:, sls[x]] * jnp.exp(gl[x] - gcc[x]) for x in (a, b)], axis=0)
        v_bd = jnp.concatenate([jnp.concatenate([v16[a], z_val], axis=1),
                                jnp.concatenate([z_val, v16[b]], axis=1)], axis=0)
        upd = _dot(k_dec.T.astype(BF16), v_bd)
        st_ref[a] = st_ref[a] * jnp.exp(gl[a]) + upd[:, 0:HEAD_DIM]
        st_ref[b] = st_ref[b] * jnp.exp(gl[b]) + upd[:, HEAD_DIM:]


def _gdn_scan(qkvn, small, abt3, a_log, dt_bias, rows, ab_block):
    r = qkvn.shape[0]
    ncc, ncl = rows.c // CHUNK, rows.l // CHUNK
    lat_blocks = rows.t // CHUNK

    def rb(fwd, b, s):
        cc = s if fwd else ncc - 1 - s
        lc = s - ncc if fwd else ncc + ncl - 1 - s
        return jnp.where(s < ncc, lat_blocks + b * ncc + cc, b * ncl + lc)

    def dir_specs(fwd):
        return [pl.BlockSpec((CHUNK, w), lambda b, s: (rb(fwd, b, s), 0)),
                pl.BlockSpec((CHUNK, w), lambda b, s: (rb(fwd, b, s), 1)),
                pl.BlockSpec((CHUNK, w), lambda b, s: (rb(fwd, b, s), 2)),
                pl.BlockSpec((CHUNK, LANES), lambda b, s: (rb(fwd, b, s), ab_block)),
                pl.BlockSpec((None, CHUNK, CHUNK), lambda b, s: (rb(fwd, b, s), 0, 0))]

    w = QK_W
    al = a_log.reshape(1, 2 * HEADS)
    dtb = dt_bias.reshape(1, 2 * HEADS)
    vec_spec = pl.BlockSpec((1, 2 * HEADS), lambda b, s: (0, 0))
    col_spec = pl.BlockSpec((2 * HEADS, 1), lambda b, s: (0, 0))
    return pl.pallas_call(
        _gdn_scan_kernel,
        grid=(rows.b, ncc + ncl),
        in_specs=dir_specs(True) + dir_specs(False) + [vec_spec, vec_spec, col_spec, col_spec],
        out_specs=(pl.BlockSpec((CHUNK, w), lambda b, s: (rb(True, b, s), 0)),
                   pl.BlockSpec((CHUNK, w), lambda b, s: (rb(False, b, s), 0))),
        out_shape=(jax.ShapeDtypeStruct((r, w), F32), jax.ShapeDtypeStruct((r, w), F32)),
        scratch_shapes=[pltpu.VMEM((2, HEADS, HEAD_DIM, HEAD_DIM), F32)],
        compiler_params=_cparams(("parallel", "arbitrary"), 40),
        name="gdn_scan",
    )(qkvn, qkvn, qkvn, small, abt3, qkvn, qkvn, qkvn, small, abt3, al, dtb, al.reshape(-1, 1), dtb.reshape(-1, 1))


def _gdn_out_kernel(of_ref, ob_ref, z_ref, g_ref, y_ref):
    o = of_ref[...] + ob_ref[...]
    z = z_ref[...]
    for h in range(HEADS):
        sl = slice(h * HEAD_DIM, (h + 1) * HEAD_DIM)
        seg = o[:, sl]
        ms = jnp.mean(seg * seg, -1, keepdims=True)
        y_ref[:, sl] = (seg * lax.rsqrt(ms + EPS) * g_ref[...] * _silu(z[:, sl])).astype(y_ref.dtype)


def _gdn_out(o_fwd, o_bwd, z, gdn_norm, m):
    w = QK_W
    return pl.pallas_call(
        _gdn_out_kernel,
        grid=(m // ROW_TILE,),
        in_specs=[pl.BlockSpec((ROW_TILE, w), lambda i: (i, 0)),
                  pl.BlockSpec((ROW_TILE, w), lambda i: (i, 0)),
                  pl.BlockSpec((ROW_TILE, w), lambda i: (i, 0)),
                  pl.BlockSpec((1, HEAD_DIM), lambda i: (0, 0))],
        out_specs=pl.BlockSpec((ROW_TILE, w), lambda i: (i, 0)),
        out_shape=jax.ShapeDtypeStruct((m, w), BF16),
        compiler_params=_cparams(("parallel",)),
        name="gdn_out",
    )(o_fwd, o_bwd, z, gdn_norm.reshape(1, HEAD_DIM))


ATTN_SUB = 256
Q_HEAD_W = 2 * HEAD_DIM
UQ_HEAD_COLS = 3 * HEAD_DIM


def _rms(x, g):
    return x * lax.rsqrt(jnp.mean(x * x, -1, keepdims=True) + EPS) * g


def _mla_q_kernel(x_ref, g_ref, w_ref, tab_ref, o_ref):
    xn = _rms(x_ref[...], g_ref[...]).astype(BF16)
    cos_s = tab_ref[:, 0:LANES]
    sin_s = tab_ref[:, LANES:2 * LANES]
    for h in range(HEADS):
        r = _dot(xn, w_ref[:, h * UQ_HEAD_COLS:(h + 1) * UQ_HEAD_COLS])
        o_ref[:, h * Q_HEAD_W:h * Q_HEAD_W + LANES] = (r[:, 0:LANES] * Q_SCALE).astype(o_ref.dtype)
        rope = r[:, LANES:2 * LANES] * cos_s + r[:, 2 * LANES:] * sin_s
        o_ref[:, h * Q_HEAD_W + LANES:(h + 1) * Q_HEAD_W] = rope.astype(o_ref.dtype)


def _mla_q(small, q_norm, w_ext, tab, m, tm):
    return pl.pallas_call(
        _mla_q_kernel,
        grid=(m // tm,),
        in_specs=[pl.BlockSpec((tm, Q_LORA), lambda i: (i, 0)),
                  pl.BlockSpec((1, Q_LORA), lambda i: (0, 0)),
                  pl.BlockSpec(w_ext.shape, lambda i: (0, 0)),
                  pl.BlockSpec((tm, 4 * LANES), lambda i: (i, 0))],
        out_specs=pl.BlockSpec((tm, HEADS * Q_HEAD_W), lambda i: (i, 0)),
        out_shape=jax.ShapeDtypeStruct((m, HEADS * Q_HEAD_W), BF16),
        compiler_params=_cparams(("parallel",), 48),
        name="mla_q",
    )(small, q_norm.reshape(1, Q_LORA), w_ext, tab)


def _mla_kv_kernel(x_ref, kr_ref, g_ref, w_ref, tab_ref, k_ref, v_ref):
    xn = _rms(x_ref[...], g_ref[...]).astype(BF16)
    kr = kr_ref[...]
    rope = kr[:, 0:LANES] * tab_ref[:, 2 * LANES:3 * LANES] + kr[:, LANES:] * tab_ref[:, 3 * LANES:]
    rope = rope.astype(k_ref.dtype)
    for h in range(HEADS):
        r = _dot(xn, w_ref[:, h * Q_HEAD_W:(h + 1) * Q_HEAD_W])
        k_ref[:, h * Q_HEAD_W:h * Q_HEAD_W + LANES] = r[:, 0:LANES].astype(k_ref.dtype)
        k_ref[:, h * Q_HEAD_W + LANES:(h + 1) * Q_HEAD_W] = rope
        v_ref[:, h * HEAD_DIM:(h + 1) * HEAD_DIM] = r[:, LANES:].astype(v_ref.dtype)


def _mla_kv(small, kv_norm, w_ukv, tab, m, tm):
    return pl.pallas_call(
        _mla_kv_kernel,
        grid=(m // tm,),
        in_specs=[pl.BlockSpec((tm, KV_LORA), lambda i: (i, 1)),
                  pl.BlockSpec((tm, 2 * LANES), lambda i: (i, 4)),
                  pl.BlockSpec((1, KV_LORA), lambda i: (0, 0)),
                  pl.BlockSpec(w_ukv.shape, lambda i: (0, 0)),
                  pl.BlockSpec((tm, 4 * LANES), lambda i: (i, 0))],
        out_specs=(pl.BlockSpec((tm, HEADS * Q_HEAD_W), lambda i: (i, 0)),
                   pl.BlockSpec((tm, HEADS * HEAD_DIM), lambda i: (i, 0))),
        out_shape=(jax.ShapeDtypeStruct((m, HEADS * Q_HEAD_W), BF16),
                   jax.ShapeDtypeStruct((m, HEADS * HEAD_DIM), BF16)),
        compiler_params=_cparams(("parallel",), 48),
        name="mla_kv",
    )(small, small, kv_norm.reshape(1, KV_LORA), w_ukv, tab)


def _attn_kernel(*refs, with_latent):
    if with_latent:
        q_ref, kl_ref, vl_ref, kc_ref, vc_ref, o_ref = refs
    else:
        q_ref, kc_ref, vc_ref, o_ref = refs
    subs = [slice(i, i + ATTN_SUB) for i in range(0, q_ref.shape[0], ATTN_SUB)]

    def scores(s):
        q = q_ref[s, :]
        return (_dot_nt(q, kc_ref[...]), _dot_nt(q, kl_ref[...]) if with_latent else None)

    def finish(s, sc, sl):
        m = jnp.max(sc, -1, keepdims=True)
        if with_latent:
            m = jnp.maximum(m, jnp.max(sl, -1, keepdims=True))
        pc = jnp.exp2(sc - m)
        den = jnp.sum(pc, -1, keepdims=True)
        acc = _dot(pc.astype(BF16), vc_ref[...])
        if with_latent:
            p_lat = jnp.exp2(sl - m)
            den = den + jnp.sum(p_lat, -1, keepdims=True)
            acc = acc + _dot(p_lat.astype(BF16), vl_ref[...])
        o_ref[s, :] = (acc / den).astype(o_ref.dtype)

    pending = None
    for s in subs:
        nxt = (s,) + scores(s)
        if pending is not None:
            finish(*pending)
        pending = nxt
    finish(*pending)


def _attention(qf, kf, vf, rows, latent_queries, tq):
    b, l, c = rows.b, rows.l, rows.c
    ctx0 = rows.t // c
    kc_spec = pl.BlockSpec((c, Q_HEAD_W), lambda bi, h, qi: (ctx0 + bi, h))
    vc_spec = pl.BlockSpec((c, HEAD_DIM), lambda bi, h, qi: (ctx0 + bi, h))
    if latent_queries:
        nq = l // tq
        q_spec = pl.BlockSpec((tq, Q_HEAD_W), lambda bi, h, qi: (bi * nq + qi, h))
        o_spec = pl.BlockSpec((tq, HEAD_DIM), lambda bi, h, qi: (bi * nq + qi, h))
        in_specs = [q_spec,
                    pl.BlockSpec((l, Q_HEAD_W), lambda bi, h, qi: (bi, h)),
                    pl.BlockSpec((l, HEAD_DIM), lambda bi, h, qi: (bi, h)),
                    kc_spec, vc_spec]
        args = (qf, kf, vf, kf, vf)
        m_out = rows.t
    else:
        nq = c // tq
        q0 = rows.t // tq
        q_spec = pl.BlockSpec((tq, Q_HEAD_W), lambda bi, h, qi: (q0 + bi * nq + qi, h))
        o_spec = pl.BlockSpec((tq, HEAD_DIM), lambda bi, h, qi: (bi * nq + qi, h))
        in_specs = [q_spec, kc_spec, vc_spec]
        args = (qf, kf, vf)
        m_out = rows.r - rows.t
    return pl.pallas_call(
        functools.partial(_attn_kernel, with_latent=latent_queries),
        grid=(b, HEADS, nq),
        in_specs=in_specs, out_specs=o_spec,
        out_shape=jax.ShapeDtypeStruct((m_out, HEADS * HEAD_DIM), BF16),
        compiler_params=_cparams(("parallel", "parallel", "arbitrary"), 48),
        name="mla_attn",
    )(*args)


def _router_kernel(h_ref, w_ref, ri_ref, pos_ref, cnt_ref, carry_ref, *, n_experts):
    i = pl.program_id(0)

    @pl.when(i == 0)
    def _():
        carry_ref[...] = jnp.zeros_like(carry_ref)

    logits = jnp.dot(h_ref[...], w_ref[...], preferred_element_type=F32, precision=HI)
    lane = lax.broadcasted_iota(jnp.int32, logits.shape, 1).astype(F32)
    neg = -jnp.inf
    logits = jnp.where(lane < n_experts, logits, neg)
    m1 = jnp.max(logits, -1, keepdims=True)
    i1 = jnp.min(jnp.where(logits == m1, lane, float(LANES)), -1, keepdims=True)
    sel1 = lane == i1
    rest = jnp.where(sel1, neg, logits)
    m2 = jnp.max(rest, -1, keepdims=True)
    i2 = jnp.min(jnp.where(rest == m2, lane, float(LANES)), -1, keepdims=True)
    sel2 = lane == i2
    e = jnp.exp(m2 - m1)
    w1 = 1.0 / (1.0 + e)
    w2 = e / (1.0 + e)
    ri_ref[...] = jnp.where(lane == 0, i1,
                            jnp.where(lane == 1, i2,
                                      jnp.where(lane == 2, w1, jnp.where(lane == 3, w2, 0.0))))
    onehot = jnp.where(sel1, 1.0, jnp.where(sel2, 1.0, 0.0))
    tile = logits.shape[0]
    ri_t = lax.broadcasted_iota(jnp.int32, (tile, tile), 0)
    ci_t = lax.broadcasted_iota(jnp.int32, (tile, tile), 1)
    before = (ci_t < ri_t).astype(BF16)
    pos_ref[...] = _dot(before, onehot.astype(BF16)) + carry_ref[0:1, :]
    carry_ref[0:1, :] = carry_ref[0:1, :] + jnp.sum(onehot, axis=0, keepdims=True)
    cnt_ref[...] = jnp.broadcast_to(carry_ref[0:1, :], cnt_ref.shape)


def _router(hf, router_w, n_experts):
    t, d = hf.shape
    wpad = jnp.zeros((d, LANES), F32).at[:, :n_experts].set(router_w)
    return pl.pallas_call(
        functools.partial(_router_kernel, n_experts=n_experts),
        grid=(t // ROW_TILE,),
        in_specs=[pl.BlockSpec((ROW_TILE, d), lambda i: (i, 0)),
                  pl.BlockSpec((d, LANES), lambda i: (0, 0))],
        out_specs=(pl.BlockSpec((ROW_TILE, LANES), lambda i: (i, 0)),
                   pl.BlockSpec((ROW_TILE, LANES), lambda i: (i, 0)),
                   pl.BlockSpec((8, LANES), lambda i: (0, 0))),
        out_shape=(jax.ShapeDtypeStruct((t, LANES), F32),
                   jax.ShapeDtypeStruct((t, LANES), F32),
                   jax.ShapeDtypeStruct((8, LANES), F32)),
        scratch_shapes=[pltpu.VMEM((8, LANES), F32)],
        compiler_params=_cparams(("arbitrary",)),
        name="moe_router",
    )(hf, wpad)


def _row_copy(src_hbm, row, buf, r, sem):
    return pltpu.make_async_copy(src_hbm.at[pl.ds(row, 1), :], buf.at[pl.ds(r, 1), :], sem)


GATHER_UNROLL = 8


def _gather_rows(idx_ref, base, n, src_hbm, buf, sem):
    def issue(it, c):
        for u in range(GATHER_UNROLL):
            r = it * GATHER_UNROLL + u
            _row_copy(src_hbm, idx_ref[base + r], buf, r, sem).start(priority=u % 2)
        return c

    lax.fori_loop(0, n // GATHER_UNROLL, issue, 0)
    pltpu.make_async_copy(src_hbm.at[pl.ds(0, n), :], buf, sem).wait()


def _moe_gather_kernel(valid_ref, tok_ref, h_hbm, o_ref, buf, sem):
    i = pl.program_id(0)

    @pl.when(valid_ref[i] > 0)
    def _():
        _gather_rows(tok_ref, i * MOE_TILE, MOE_TILE, h_hbm, buf, sem)
        o_ref[...] = buf[...].astype(o_ref.dtype)

    @pl.when(valid_ref[i] == 0)
    def _():
        o_ref[...] = jnp.zeros_like(o_ref)


def _moe_gather(valid, tok, hf, n_slots):
    d = hf.shape[1]
    return pl.pallas_call(
        _moe_gather_kernel,
        grid_spec=pltpu.PrefetchScalarGridSpec(
            num_scalar_prefetch=2,
            grid=(n_slots // MOE_TILE,),
            in_specs=[pl.BlockSpec(memory_space=pl.ANY)],
            out_specs=pl.BlockSpec((MOE_TILE, d), lambda i, v, t: (i, 0)),
            scratch_shapes=[pltpu.VMEM((MOE_TILE, d), F32), pltpu.SemaphoreType.DMA]),
        out_shape=jax.ShapeDtypeStruct((n_slots, d), BF16),
        compiler_params=_cparams(("arbitrary",)),
        name="moe_gather",
    )(valid, tok, hf)


def _moe_up_kernel(te_ref, valid_ref, x_ref, w1_ref, w3_ref, o_ref, w1b_ref, w3b_ref):
    i = pl.program_id(1)

    @pl.when((i == 0) | (te_ref[i] != te_ref[jnp.maximum(i - 1, 0)]))
    def _():
        w1b_ref[...] = w1_ref[...].astype(BF16)
        w3b_ref[...] = w3_ref[...].astype(BF16)

    @pl.when(valid_ref[i] > 0)
    def _():
        x = x_ref[...]
        o_ref[...] = (_silu(_dot(x, w1b_ref[...])) * _dot(x, w3b_ref[...])).astype(o_ref.dtype)

    @pl.when(valid_ref[i] == 0)
    def _():
        o_ref[...] = jnp.zeros_like(o_ref)


def _moe_up(te, valid, xs, w1, w3, layer, tf):
    s, d = xs.shape
    f = w1.shape[3]
    w_spec = pl.BlockSpec((None, None, d, tf), lambda j, i, te, v: (layer, te[i], 0, j))
    return pl.pallas_call(
        _moe_up_kernel,
        grid_spec=pltpu.PrefetchScalarGridSpec(
            num_scalar_prefetch=2,
            grid=(f // tf, s // MOE_TILE),
            in_specs=[pl.BlockSpec((MOE_TILE, d), lambda j, i, te, v: (i, 0)), w_spec, w_spec],
            out_specs=pl.BlockSpec((MOE_TILE, tf), lambda j, i, te, v: (i, j)),
            scratch_shapes=[pltpu.VMEM((d, tf), BF16), pltpu.VMEM((d, tf), BF16)]),
        out_shape=jax.ShapeDtypeStruct((s, f), BF16),
        compiler_params=_cparams(("arbitrary", "arbitrary"), 48),
        name="moe_up",
    )(te, valid, xs, w1, w3)


def _moe_down_kernel(te_ref, valid_ref, x_ref, w_ref, o_ref):
    i = pl.program_id(1)

    @pl.when(valid_ref[i] > 0)
    def _():
        o_ref[...] = _dot(x_ref[...], w_ref[...])

    @pl.when(valid_ref[i] == 0)
    def _():
        o_ref[...] = jnp.zeros_like(o_ref)


def _moe_down(te, valid, hmid, w2, tn):
    s, f = hmid.shape
    d = w2.shape[2]
    return pl.pallas_call(
        _moe_down_kernel,
        grid_spec=pltpu.PrefetchScalarGridSpec(
            num_scalar_prefetch=2,
            grid=(d // tn, s // MOE_TILE),
            in_specs=[pl.BlockSpec((MOE_TILE, f), lambda j, i, te, v: (i, 0)),
                      pl.BlockSpec((None, f, tn), lambda j, i, te, v: (te[i], 0, j))],
            out_specs=pl.BlockSpec((MOE_TILE, tn), lambda j, i, te, v: (i, j))),
        out_shape=jax.ShapeDtypeStruct((s, d), F32),
        compiler_params=_cparams(("arbitrary", "arbitrary"), 48),
        name="moe_down",
    )(te, valid, hmid, w2)


def _moe_combine_kernel(s1_ref, s2_ref, y_hbm, ri_ref, x_ref, gt_ref, g_ref, b_ref, o_ref,
                        buf1, buf2, sem1, sem2, *, alpha):
    i = pl.program_id(0)
    base = i * ROW_TILE

    def issue(it, c):
        for u in range(GATHER_UNROLL):
            r = it * GATHER_UNROLL + u
            _row_copy(y_hbm, s1_ref[base + r], buf1, r, sem1).start(priority=0)
            _row_copy(y_hbm, s2_ref[base + r], buf2, r, sem2).start(priority=1)
        return c

    lax.fori_loop(0, ROW_TILE // GATHER_UNROLL, issue, 0)
    pltpu.make_async_copy(y_hbm.at[pl.ds(0, ROW_TILE), :], buf1, sem1).wait()
    pltpu.make_async_copy(y_hbm.at[pl.ds(0, ROW_TILE), :], buf2, sem2).wait()
    ri = ri_ref[...]
    y = ri[:, 2:3] * buf1[...] + ri[:, 3:4] * buf2[...]
    xn = _layer_norm(alpha * x_ref[...] + gt_ref[...] * y)
    o_ref[...] = xn * g_ref[...] + b_ref[...]


def _moe_combine(slot1, slot2, yslot, ri, x, mods, gate_piece, ln_g, ln_b, rows, alpha):
    t, d = x.shape
    row_spec = pl.BlockSpec((ROW_TILE, d), lambda i, a, b: (i, 0))
    vec_spec = pl.BlockSpec((1, d), lambda i, a, b: (0, 0))
    return pl.pallas_call(
        functools.partial(_moe_combine_kernel, alpha=alpha),
        grid_spec=pltpu.PrefetchScalarGridSpec(
            num_scalar_prefetch=2,
            grid=(t // ROW_TILE,),
            in_specs=[pl.BlockSpec(memory_space=pl.ANY),
                      pl.BlockSpec((ROW_TILE, LANES), lambda i, a, b: (i, 0)),
                      row_spec,
                      pl.BlockSpec((None, None, 1, d), lambda i, a, b: (gate_piece, rows.group(i), 0, 0)),
                      vec_spec, vec_spec],
            out_specs=row_spec,
            scratch_shapes=[pltpu.VMEM((ROW_TILE, d), F32), pltpu.VMEM((ROW_TILE, d), F32),
                            pltpu.SemaphoreType.DMA, pltpu.SemaphoreType.DMA]),
        out_shape=jax.ShapeDtypeStruct((t, d), F32),
        compiler_params=_cparams(("arbitrary",)),
        name="moe_combine",
    )(slot1, slot2, yslot, ri, x, mods, ln_g, ln_b)


def _moe_plan(ri, pos, cnt, n_experts, n_slots):
    t = ri.shape[0]
    i1 = ri[:, 0].astype(jnp.int32)
    i2 = ri[:, 1].astype(jnp.int32)
    pos8 = pos[:, :n_experts].astype(jnp.int32)
    counts = cnt[0, :n_experts].astype(jnp.int32)
    padded = ((counts + MOE_TILE - 1) // MOE_TILE) * MOE_TILE
    ends = jnp.cumsum(padded)
    offs = ends - padded
    slot1 = offs[i1] + jnp.take_along_axis(pos8, i1[:, None], axis=1)[:, 0]
    slot2 = offs[i2] + jnp.take_along_axis(pos8, i2[:, None], axis=1)[:, 0]
    ids = jnp.arange(t, dtype=jnp.int32)
    tok = jnp.zeros((n_slots,), jnp.int32).at[slot1].set(ids).at[slot2].set(ids)
    starts = jnp.arange(n_slots // MOE_TILE, dtype=jnp.int32) * MOE_TILE
    valid = (starts < ends[-1]).astype(jnp.int32)
    te = jnp.minimum(jnp.searchsorted(ends, starts, side="right").astype(jnp.int32), n_experts - 1)
    te = jnp.where(valid > 0, te, jnp.max(jnp.where(valid > 0, te, 0)))
    return slot1, slot2, tok, te, valid


def _rope_tables(rows):
    l = rows.l
    nrow = l // GRID_W
    row = jnp.repeat(jnp.arange(nrow), GRID_W).astype(F32)
    col = jnp.tile(jnp.arange(GRID_W), nrow).astype(F32)
    n_freq = ROPE_DIM // 4
    inv_freq = ROPE_THETA ** (-jnp.arange(n_freq, dtype=F32) / n_freq)
    cr, sr = jnp.cos(row[:, None] * inv_freq), jnp.sin(row[:, None] * inv_freq)
    cc, sc = jnp.cos(col[:, None] * inv_freq), jnp.sin(col[:, None] * inv_freq)
    cos = jnp.concatenate([cr, cr, cc, cc], -1)
    sin = jnp.concatenate([-sr, sr, -sc, sc], -1)
    nctx = rows.r - rows.t
    cos = jnp.concatenate([jnp.tile(cos, (rows.b, 1)), jnp.ones((nctx, ROPE_DIM), F32)], 0)
    sin = jnp.concatenate([jnp.tile(sin, (rows.b, 1)), jnp.zeros((nctx, ROPE_DIM), F32)], 0)
    z = jnp.zeros((rows.r, LANES - ROPE_DIM), F32)
    return jnp.concatenate([Q_SCALE * cos, z, Q_SCALE * sin, z, cos, z, sin, z], -1)


_ROPE_SWAP = np.concatenate([np.arange(16, 32), np.arange(0, 16), np.arange(48, 64), np.arange(32, 48)])


def _uq_ext(w_uq):
    k = w_uq.shape[0]
    w = w_uq.reshape(k, HEADS, HEAD_DIM + ROPE_DIM)
    nope, rope = w[:, :, :HEAD_DIM], w[:, :, HEAD_DIM:]
    z = jnp.zeros((k, HEADS, LANES - ROPE_DIM), w.dtype)
    ext = jnp.concatenate([nope, rope, z, rope[:, :, _ROPE_SWAP], z], -1)
    return ext.reshape(k, HEADS * UQ_HEAD_COLS).astype(BF16)


def kernel(x, c, ctx, c_ctx, w_mod, b_mod, w_in, conv_w, a_log, dt_bias, gdn_norm, q_norm, kv_norm, w_uq, w_ukv, w_br_a, w_br_b, w_out, ln1_g, ln1_b, ln2_g, ln2_b, ffn_w1, ffn_w3, ffn_w2, moe_router, moe_w1, moe_w3, moe_w2):
    bsz, n_lat, d = x.shape
    n_ctx = ctx.shape[1]
    depth = w_mod.shape[0]
    assert depth == 2 and ffn_w1.shape[0] == 1 and moe_w1.shape[0] == 1
    rows = _Rows(bsz, n_lat, n_ctx)
    r_all, t_lat = rows.r, rows.t
    alpha = (2 * depth) ** 0.25
    n_experts = moe_router.shape[2]

    xs = jnp.concatenate([x.reshape(t_lat, d), ctx.reshape(r_all - t_lat, d)], 0)
    n_groups = -(-(bsz + 1) // 8) * 8
    cs = jnp.zeros((n_groups, d), F32).at[:bsz].set(c).at[bsz].set(c_ctx)
    tab = _rope_tables(rows)

    o_z = 3 * QK_W
    o_a = o_z + QK_W
    o_b = o_a + 2 * HEADS
    o_dq = o_b + 2 * HEADS
    o_dkv = o_dq + Q_LORA
    o_kr = o_dkv + KV_LORA
    o_ga = o_kr + ROPE_DIM
    ab_block = (Q_LORA + KV_LORA + 2 * LANES) // LANES

    tm_all = _pick(r_all, (1024, 512, 256))
    tm_lat = _pick(t_lat, (1024, 512, 256))
    tm_half = _pick(r_all, (512, 256))

    all_mods = [_mod_params(cs, w_mod, b_mod, i).reshape(n_groups, 6, d).transpose(1, 0, 2)[:, :, None, :]
                for i in range(depth)]

    h = None
    for i in range(depth):
        last = i == depth - 1
        m_mix = t_lat if last else r_all
        tm_mix = tm_lat if last else tm_all
        mods = all_mods[i]
        if i == 0:
            h = _ln_mod(xs, mods, rows, r_all, 0, 1, BF16)

        wi = w_in[i]
        kr_w = wi[:, o_kr:o_ga]
        zc = jnp.zeros((d, LANES - ROPE_DIM), F32)
        w_small = jnp.concatenate(
            [wi[:, o_dq:o_kr], kr_w, zc, kr_w[:, _ROPE_SWAP], zc, wi[:, o_a:o_dq], zc], 1).astype(BF16)
        qkv = _mm(h, wi[:, :o_z].astype(BF16), r_all, F32, tm_all, _pick(o_z, (1536, 1024, 512)))
        small = _mm(h, w_small, r_all, F32, tm_all, w_small.shape[1])
        z = _mm(h, wi[:, o_z:o_a].astype(BF16), m_mix, F32, tm_mix, 1024)
        gates = _mm(h, wi[:, o_ga:].astype(BF16), m_mix, F32, tm_mix, 1024)

        qkvn = _gdn_prep(qkv, conv_w[i], rows)
        ab = small[:, ab_block * LANES:ab_block * LANES + 4 * HEADS]
        abt3 = ab.reshape(r_all // CHUNK, CHUNK, 4 * HEADS).transpose(0, 2, 1)
        o_fwd, o_bwd = _gdn_scan(qkvn, small, abt3, a_log[i], dt_bias[i], rows, ab_block)
        ya = _gdn_out(o_fwd, o_bwd, z, gdn_norm[i], m_mix)

        qf = _mla_q(small, q_norm[i], _uq_ext(w_uq[i]), tab, m_mix, _pick(m_mix, (512, 256)))
        kf, vf = _mla_kv(small, kv_norm[i], w_ukv[i].astype(BF16), tab, r_all, tm_half)
        yb = _attention(qf, kf, vf, rows, True, _pick(n_lat, (1024, 512, 256)))
        if not last:
            yb_ctx = _attention(qf, kf, vf, rows, False, _pick(n_ctx, (256,)))
            yb = jnp.concatenate([yb, yb_ctx], 0)

        ym = _merge(ya, yb, w_br_a[i].astype(BF16), w_br_b[i].astype(BF16), gates, m_mix,
                    _pick(m_mix, (512, 256)), 1024)
        mres = _mm(ym, w_out[i].astype(BF16), m_mix, F32, tm_mix, 1024)
        moe_layer = i % 2 == 1
        x1, h2 = _resid_ln(xs, mres, mods, 2, ln1_g[i].reshape(1, d), ln1_b[i].reshape(1, d), rows, m_mix,
                           alpha, next_mods=(mods, 3, 4), next_dtype=F32 if moe_layer else BF16)

        g2, b2 = ln2_g[i].reshape(1, d), ln2_b[i].reshape(1, d)
        if not moe_layer:
            j = i // 2
            hmid = _swiglu_up(h2, ffn_w1[j].astype(BF16), ffn_w3[j].astype(BF16), m_mix, tm_mix, 512)
            f_out = _mm(hmid, ffn_w2[j].astype(BF16), m_mix, F32, _pick(m_mix, (512, 256)), 1024, n_outer=True)
            if last:
                xs = _resid_ln(x1, f_out, mods, 5, g2, b2, rows, m_mix, alpha)
            else:
                xs, h = _resid_ln(x1, f_out, mods, 5, g2, b2, rows, m_mix, alpha,
                                  next_mods=(all_mods[i + 1], 0, 1))
        else:
            assert last
            j = i // 2
            ri, pos, cnt = _router(h2, moe_router[j], n_experts)
            n_slots = TOP_K * t_lat + n_experts * MOE_TILE
            slot1, slot2, tok, te, valid = _moe_plan(ri, pos, cnt, n_experts, n_slots)
            xg = _moe_gather(valid, tok, h2, n_slots)
            hmid = _moe_up(te, valid, xg, moe_w1, moe_w3, j, 512)
            yslot = _moe_down(te, valid, hmid, moe_w2[j].astype(BF16), 512)
            xs = _moe_combine(slot1, slot2, yslot, ri, x1, mods, 5, g2, b2, rows, alpha)
    return xs[:t_lat].reshape(bsz, n_lat, d)
```

```python
import functools
import math

import numpy as np
import jax
import jax.numpy as jnp
from jax import lax
from jax.experimental import pallas as pl
from jax.experimental.pallas import tpu as pltpu

F32 = jnp.float32
BF16 = jnp.bfloat16
HI = lax.Precision.HIGHEST

HEADS = 16
HEAD_DIM = 128
ROPE_DIM = 64
Q_LORA = 512
KV_LORA = 512
CHUNK = 64
CONV_K = 5
GRID_W = 64
ROPE_THETA = 10000.0
QK_W = HEADS * HEAD_DIM
MLA_SCALE = (HEAD_DIM + ROPE_DIM) ** -0.5
Q_SCALE = MLA_SCALE * math.log2(math.e)
TOP_K = 2
EPS = 1e-6
ROW_TILE = 256
MOE_TILE = 512
LANES = 128


def _cparams(sem, vmem_mb=None):
    kw = dict(dimension_semantics=sem)
    if vmem_mb is not None:
        kw["vmem_limit_bytes"] = vmem_mb << 20
    return pltpu.CompilerParams(**kw)


def _pick(m, cands):
    for c in cands:
        if m % c == 0:
            return c
    raise ValueError(f"no tile for {m}")


def _dot(a, b):
    return jnp.dot(a, b, preferred_element_type=F32)


def _dot_nt(a, b):
    return lax.dot_general(a, b, (((1,), (1,)), ((), ())), preferred_element_type=F32)


def _silu(x):
    return x * jax.nn.sigmoid(x)


def _mod_kernel(c_ref, w_ref, b_ref, o_ref):
    s = _silu(c_ref[...])
    o_ref[...] = jnp.dot(s, w_ref[...], preferred_element_type=F32, precision=HI) + b_ref[...]


def _mod_params(cs, w_mod, b_mod, layer):
    g, d = cs.shape
    n = w_mod.shape[2]
    tn = _pick(n, (1024, 512))
    return pl.pallas_call(
        _mod_kernel,
        grid=(n // tn,),
        in_specs=[pl.BlockSpec((g, d), lambda j: (0, 0)),
                  pl.BlockSpec((None, d, tn), lambda j: (layer, 0, j)),
                  pl.BlockSpec((None, 1, tn), lambda j: (layer, 0, j))],
        out_specs=pl.BlockSpec((g, tn), lambda j: (0, j)),
        out_shape=jax.ShapeDtypeStruct((g, n), F32),
        compiler_params=_cparams(("arbitrary",), 40),
        name="mod_params",
    )(cs, w_mod, b_mod.reshape(b_mod.shape[0], 1, n))


def _layer_norm(x):
    mu = jnp.mean(x, -1, keepdims=True)
    xc = x - mu
    var = jnp.mean(xc * xc, -1, keepdims=True)
    return xc * lax.rsqrt(var + EPS)


def _ln_mod_kernel(x_ref, sh_ref, sc_ref, o_ref):
    y = _layer_norm(x_ref[...])
    o_ref[...] = (y * (1.0 + sc_ref[...]) + sh_ref[...]).astype(o_ref.dtype)


def _resid_ln_kernel(x_ref, m_ref, gt_ref, g_ref, b_ref, *rest, alpha, with_mod):
    y = _layer_norm(alpha * x_ref[...] + gt_ref[...] * m_ref[...])
    xn = y * g_ref[...] + b_ref[...]
    if with_mod:
        sh_ref, sc_ref, xo_ref, ho_ref = rest
        xo_ref[...] = xn
        ho_ref[...] = (_layer_norm(xn) * (1.0 + sc_ref[...]) + sh_ref[...]).astype(ho_ref.dtype)
    else:
        (xo_ref,) = rest
        xo_ref[...] = xn


def _group_of_tile(i, lat_tiles, tiles_per_seq, n_batch):
    return jnp.where(i < lat_tiles, i // tiles_per_seq, n_batch)


class _Rows:
    def __init__(self, n_batch, n_lat, n_ctx):
        self.b, self.l, self.c = n_batch, n_lat, n_ctx
        self.t = n_batch * n_lat
        self.r = n_batch * (n_lat + n_ctx)
        self.lat_tiles = self.t // ROW_TILE
        self.lat_tps = n_lat // ROW_TILE
        self.ctx_tps = n_ctx // ROW_TILE

    def group(self, i):
        return _group_of_tile(i, self.lat_tiles, self.lat_tps, self.b)


def _mod_spec(rows, piece, d):
    return pl.BlockSpec((None, None, 1, d), lambda i: (piece, rows.group(i), 0, 0))


def _ln_mod(x, mods, rows, m, shift_piece, scale_piece, out_dtype):
    d = x.shape[1]
    return pl.pallas_call(
        _ln_mod_kernel,
        grid=(m // ROW_TILE,),
        in_specs=[pl.BlockSpec((ROW_TILE, d), lambda i: (i, 0)),
                  _mod_spec(rows, shift_piece, d), _mod_spec(rows, scale_piece, d)],
        out_specs=pl.BlockSpec((ROW_TILE, d), lambda i: (i, 0)),
        out_shape=jax.ShapeDtypeStruct((m, d), out_dtype),
        compiler_params=_cparams(("parallel",)),
        name="ln_mod",
    )(x, mods, mods)


def _resid_ln(x, mres, mods, gate_piece, ln_g, ln_b, rows, m, alpha, next_mods=None, next_dtype=BF16):
    d = x.shape[1]
    row_spec = pl.BlockSpec((ROW_TILE, d), lambda i: (i, 0))
    vec_spec = pl.BlockSpec((1, d), lambda i: (0, 0))
    in_specs = [row_spec, row_spec, _mod_spec(rows, gate_piece, d), vec_spec, vec_spec]
    args = [x, mres, mods, ln_g, ln_b]
    if next_mods is None:
        out_specs = row_spec
        out_shape = jax.ShapeDtypeStruct((m, d), F32)
    else:
        nm, sh_piece, sc_piece = next_mods
        in_specs += [_mod_spec(rows, sh_piece, d), _mod_spec(rows, sc_piece, d)]
        args += [nm, nm]
        out_specs = (row_spec, row_spec)
        out_shape = (jax.ShapeDtypeStruct((m, d), F32), jax.ShapeDtypeStruct((m, d), next_dtype))
    return pl.pallas_call(
        functools.partial(_resid_ln_kernel, alpha=alpha, with_mod=next_mods is not None),
        grid=(m // ROW_TILE,),
        in_specs=in_specs, out_specs=out_specs, out_shape=out_shape,
        compiler_params=_cparams(("parallel",)),
        name="resid_ln",
    )(*args)


def _mm_kernel(x_ref, w_ref, o_ref):
    o_ref[...] = _dot(x_ref[...], w_ref[...]).astype(o_ref.dtype)


def _mm(x, w, m, out_dtype, tm, tn, n_outer=False, vmem_mb=48):
    k, n = w.shape
    if n_outer:
        grid = (n // tn, m // tm)
        xi, wi, oi = (lambda j, i: (i, 0)), (lambda j, i: (0, j)), (lambda j, i: (i, j))
    else:
        grid = (m // tm, n // tn)
        xi, wi, oi = (lambda i, j: (i, 0)), (lambda i, j: (0, j)), (lambda i, j: (i, j))
    return pl.pallas_call(
        _mm_kernel,
        grid=grid,
        in_specs=[pl.BlockSpec((tm, k), xi), pl.BlockSpec((k, tn), wi)],
        out_specs=pl.BlockSpec((tm, tn), oi),
        out_shape=jax.ShapeDtypeStruct((m, n), out_dtype),
        compiler_params=_cparams(("parallel", "parallel"), vmem_mb),
        name="mm",
    )(x, w)


def _swiglu_up_kernel(x_ref, w1_ref, w3_ref, o_ref):
    x = x_ref[...]
    o_ref[...] = (_silu(_dot(x, w1_ref[...])) * _dot(x, w3_ref[...])).astype(o_ref.dtype)


def _swiglu_up(x, w1, w3, m, tm, tf):
    k, f = w1.shape
    return pl.pallas_call(
        _swiglu_up_kernel,
        grid=(m // tm, f // tf),
        in_specs=[pl.BlockSpec((tm, k), lambda i, j: (i, 0)),
                  pl.BlockSpec((k, tf), lambda i, j: (0, j)),
                  pl.BlockSpec((k, tf), lambda i, j: (0, j))],
        out_specs=pl.BlockSpec((tm, tf), lambda i, j: (i, j)),
        out_shape=jax.ShapeDtypeStruct((m, f), BF16),
        compiler_params=_cparams(("parallel", "parallel"), 48),
        name="swiglu_up",
    )(x, w1, w3)


def _merge_kernel(ya_ref, yb_ref, wa_ref, wb_ref, ga_ref, gb_ref, o_ref):
    a = _dot(ya_ref[...], wa_ref[...])
    b = _dot(yb_ref[...], wb_ref[...])
    o_ref[...] = (jax.nn.sigmoid(ga_ref[...]) * a + jax.nn.sigmoid(gb_ref[...]) * b).astype(o_ref.dtype)


def _merge(ya, yb, wa, wb, gates, m, tm, tn):
    k, n = wa.shape
    nb = n // tn
    return pl.pallas_call(
        _merge_kernel,
        grid=(nb, m // tm),
        in_specs=[pl.BlockSpec((tm, k), lambda j, i: (i, 0)),
                  pl.BlockSpec((tm, k), lambda j, i: (i, 0)),
                  pl.BlockSpec((k, tn), lambda j, i: (0, j)),
                  pl.BlockSpec((k, tn), lambda j, i: (0, j)),
                  pl.BlockSpec((tm, tn), lambda j, i: (i, j)),
                  pl.BlockSpec((tm, tn), lambda j, i: (i, nb + j))],
        out_specs=pl.BlockSpec((tm, tn), lambda j, i: (i, j)),
        out_shape=jax.ShapeDtypeStruct((m, n), BF16),
        compiler_params=_cparams(("parallel", "parallel"), 48),
        name="merge",
    )(ya, yb, wa, wb, gates, gates)


CONV_COLS = 2048
HALO = 8
INV_BLOCK = 16
assert CHUNK == 4 * INV_BLOCK


def _gdn_prep_kernel(x_ref, p_ref, n_ref, w_ref, o_ref, buf_ref, *, rows):
    i = pl.program_id(0)
    j = pl.program_id(1)
    is_lat = i < rows.lat_tiles
    pos = jnp.where(is_lat, i % rows.lat_tps, (i - rows.lat_tiles) % rows.ctx_tps)
    tps = jnp.where(is_lat, rows.lat_tps, rows.ctx_tps)
    buf_ref[0:HALO, :] = jnp.where(pos == 0, 0.0, p_ref[...])
    buf_ref[HALO:HALO + ROW_TILE, :] = x_ref[...]
    buf_ref[HALO + ROW_TILE:, :] = jnp.where(pos == tps - 1, 0.0, n_ref[...])
    base = HALO - CONV_K // 2
    acc = w_ref[0:1, :] * buf_ref[base:base + ROW_TILE, :]
    for t in range(1, CONV_K):
        acc = acc + w_ref[t:t + 1, :] * buf_ref[base + t:base + t + ROW_TILE, :]
    y = _silu(acc)
    kind = j // (QK_W // CONV_COLS)
    scale = jnp.where(kind == 0, HEAD_DIM ** -0.5, 1.0)
    for hh in range(CONV_COLS // HEAD_DIM):
        sl = slice(hh * HEAD_DIM, (hh + 1) * HEAD_DIM)
        seg = y[:, sl]
        ss = jnp.sum(seg * seg, -1, keepdims=True)
        nrm = seg * (lax.rsqrt(ss + EPS) * scale)
        o_ref[:, sl] = jnp.where(kind == 2, seg, nrm)


def _gdn_prep(qkv, conv_w, rows):
    r, n = qkv.shape
    hb = ROW_TILE // HALO
    last = r // HALO - 1
    return pl.pallas_call(
        functools.partial(_gdn_prep_kernel, rows=rows),
        grid=(r // ROW_TILE, n // CONV_COLS),
        in_specs=[pl.BlockSpec((ROW_TILE, CONV_COLS), lambda i, j: (i, j)),
                  pl.BlockSpec((HALO, CONV_COLS), lambda i, j: (jnp.maximum(i * hb - 1, 0), j)),
                  pl.BlockSpec((HALO, CONV_COLS), lambda i, j: (jnp.minimum((i + 1) * hb, last), j)),
                  pl.BlockSpec((CONV_K, CONV_COLS), lambda i, j: (0, j))],
        out_specs=pl.BlockSpec((ROW_TILE, CONV_COLS), lambda i, j: (i, j)),
        out_shape=jax.ShapeDtypeStruct((r, n), F32),
        scratch_shapes=[pltpu.VMEM((ROW_TILE + 2 * HALO, CONV_COLS), F32)],
        compiler_params=_cparams(("parallel", "parallel")),
        name="gdn_prep",
    )(qkv, qkv, qkv, conv_w)


def _softplus(x):
    return jnp.maximum(x, 0.0) + jnp.log(1.0 + jnp.exp(-jnp.abs(x)))


def _gdn_scan_kernel(qf_ref, kf_ref, vf_ref, abf_ref, abtf_ref, qb_ref, kb_ref, vb_ref, abb_ref, abtb_ref,
                     al_ref, dtb_ref, alt_ref, dtbt_ref, of_ref, ob_ref, st_ref):
    @pl.when(pl.program_id(1) == 0)
    def _():
        st_ref[...] = jnp.zeros_like(st_ref)

    shared = (al_ref, dtb_ref, alt_ref, dtbt_ref)
    pending = [_scan_direction(True, qf_ref, kf_ref, vf_ref, abf_ref, abtf_ref, *shared, of_ref, st_ref.at[0]),
               _scan_direction(False, qb_ref, kb_ref, vb_ref, abb_ref, abtb_ref, *shared, ob_ref, st_ref.at[1])]
    while pending:
        for gen in list(pending):
            try:
                next(gen)
            except StopIteration:
                pending.remove(gen)


def _scan_direction(fwd, q_ref, k_ref, v_ref, ab_ref, abt_ref, al_ref, dtb_ref, alt_ref, dtbt_ref, o_ref, st_ref):
    nh = HEADS
    dcols = slice(0, nh) if fwd else slice(nh, 2 * nh)
    ab = ab_ref[...]
    g_all = -jnp.exp(al_ref[...]) * _softplus(ab[:, 0:2 * nh] + dtb_ref[...])
    g_d = g_all[:, dcols]
    beta_d = jax.nn.sigmoid(ab[:, 2 * nh:4 * nh])[:, dcols]
    abt = abt_ref[...]
    gt_d = (-jnp.exp(alt_ref[...]) * _softplus(abt[0:2 * nh, :] + dtbt_ref[...]))[dcols, :]

    ri = lax.broadcasted_iota(jnp.int32, (CHUNK, CHUNK), 0)
    ci = lax.broadcasted_iota(jnp.int32, (CHUNK, CHUNK), 1)
    order = (ri - ci) if fwd else (ci - ri)
    incl = order >= 0
    strict = order > 0
    eye = (ri == ci).astype(F32)
    shift = INV_BLOCK.bit_length() - 1
    blk_xor = lax.shift_right_logical(ri, shift) ^ lax.shift_right_logical(ci, shift)
    blk_a = blk_xor == 0
    off_b = blk_xor == 1
    off_c = blk_xor >= 2
    gc_col = jnp.dot(incl.astype(F32), g_d, preferred_element_type=F32, precision=HI)
    gc_row = jnp.dot(gt_d, (order <= 0).astype(F32), preferred_element_type=F32, precision=HI)
    gl_row = jnp.sum(g_d, axis=0, keepdims=True)
    yield

    hs = range(nh)
    sls = [slice(h * HEAD_DIM, (h + 1) * HEAD_DIM) for h in hs]
    gcc = [gc_col[:, h:h + 1] for h in hs]
    bc = [beta_d[:, h:h + 1] for h in hs]
    gl = [gl_row[:, h:h + 1] for h in hs]
    decay = [jnp.where(incl, jnp.exp(gcc[h] - gc_row[h:h + 1, :]), 0.0) for h in hs]
    kb = [k_ref[:, sls[h]] * bc[h] for h in hs]
    kkqk = [_dot_nt(jnp.concatenate([kb[h], q_ref[:, sls[h]]], axis=0).astype(BF16),
                    k_ref[:, sls[h]].astype(BF16)) for h in hs]
    yield
    lmat = [jnp.where(strict, kkqk[h][0:CHUNK] * decay[h], 0.0) for h in hs]
    a_intra = [(kkqk[h][CHUNK:] * decay[h]).astype(BF16) for h in hs]
    p16 = [(-jnp.where(blk_a, lmat[h], 0.0)).astype(BF16) for h in hs]
    inv = [eye - jnp.where(blk_a, lmat[h], 0.0) for h in hs]
    for _ in range(INV_BLOCK.bit_length() - 2):
        p16 = [_dot(p16[h], p16[h]).astype(BF16) for h in hs]
        yield
        inv = [inv[h] + _dot(inv[h].astype(BF16), p16[h]) for h in hs]
        yield
    for off in (off_b, off_c):
        inv16 = [inv[h].astype(BF16) for h in hs]
        t = [_dot(jnp.where(off, lmat[h], 0.0).astype(BF16), inv16[h]).astype(BF16) for h in hs]
        yield
        inv = [inv[h] - _dot(inv16[h], t[h]) for h in hs]
        yield
    eg = [jnp.exp(gcc[h]) for h in hs]
    uw = [_dot(inv[h].astype(BF16),
               jnp.concatenate([v_ref[:, sls[h]] * bc[h], kb[h] * eg[h]], axis=1).astype(BF16)) for h in hs]
    yield
    z_state = jnp.zeros((HEAD_DIM, HEAD_DIM), BF16)
    z_val = jnp.zeros((CHUNK, HEAD_DIM), BF16)
    pairs = [(h, h + 1) for h in range(0, nh, 2)]
    wq = []
    for a, b in pairs:
        lhs = jnp.concatenate([jnp.concatenate([uw[x][:, HEAD_DIM:], q_ref[:, sls[x]] * eg[x]], axis=0)
                               for x in (a, b)], axis=1).astype(BF16)
        s_bd = jnp.concatenate([jnp.concatenate([st_ref[a].astype(BF16), z_state], axis=1),
                                jnp.concatenate([z_state, st_ref[b].astype(BF16)], axis=1)], axis=0)
        wq.append(_dot(lhs, s_bd))
    yield
    v16 = []
    for i, (a, b) in enumerate(pairs):
        v16.append((uw[a][:, 0:HEAD_DIM] - wq[i][0:CHUNK, 0:HEAD_DIM]).astype(BF16))
        v16.append((uw[b][:, 0:HEAD_DIM] - wq[i][0:CHUNK, HEAD_DIM:]).astype(BF16))
    for i, (a, b) in enumerate(pairs):
        o_ref[:, sls[a]] = wq[i][CHUNK:, 0:HEAD_DIM] + _dot(a_intra[a], v16[a])
        o_ref[:, sls[b]] = wq[i][CHUNK:, HEAD_DIM:] + _dot(a_intra[b], v16[b])
    yield
    for a, b in pairs:
        k_dec = jnp.concatenate([k_ref[:, sls[x]] * jnp.exp(gl[x] - gcc[x]) for x in (a, b)], axis=0)
        v_bd = jnp.concatenate([jnp.concatenate([v16[a], z_val], axis=1),
                                jnp.concatenate([z_val, v16[b]], axis=1)], axis=0)
        upd = _dot(k_dec.T.astype(BF16), v_bd)
        st_ref[a] = st_ref[a] * jnp.exp(gl[a]) + upd[:, 0:HEAD_DIM]
        st_ref[b] = st_ref[b] * jnp.exp(gl[b]) + upd[:, HEAD_DIM:]


def _gdn_scan(qkvn, small, abt3, a_log, dt_bias, rows, ab_block):
    r = qkvn.shape[0]
    ncc, ncl = rows.c // CHUNK, rows.l // CHUNK
    lat_blocks = rows.t // CHUNK

    def rb(fwd, b, s):
        cc = s if fwd else ncc - 1 - s
        lc = s - ncc if fwd else ncc + ncl - 1 - s
        return jnp.where(s < ncc, lat_blocks + b * ncc + cc, b * ncl + lc)

    def dir_specs(fwd):
        return [pl.BlockSpec((CHUNK, w), lambda b, s: (rb(fwd, b, s), 0)),
                pl.BlockSpec((CHUNK, w), lambda b, s: (rb(fwd, b, s), 1)),
                pl.BlockSpec((CHUNK, w), lambda b, s: (rb(fwd, b, s), 2)),
                pl.BlockSpec((CHUNK, LANES), lambda b, s: (rb(fwd, b, s), ab_block)),
                pl.BlockSpec((None, CHUNK, CHUNK), lambda b, s: (rb(fwd, b, s), 0, 0))]

    w = QK_W
    al = a_log.reshape(1, 2 * HEADS)
    dtb = dt_bias.reshape(1, 2 * HEADS)
    vec_spec = pl.BlockSpec((1, 2 * HEADS), lambda b, s: (0, 0))
    col_spec = pl.BlockSpec((2 * HEADS, 1), lambda b, s: (0, 0))
    return pl.pallas_call(
        _gdn_scan_kernel,
        grid=(rows.b, ncc + ncl),
        in_specs=dir_specs(True) + dir_specs(False) + [vec_spec, vec_spec, col_spec, col_spec],
        out_specs=(pl.BlockSpec((CHUNK, w), lambda b, s: (rb(True, b, s), 0)),
                   pl.BlockSpec((CHUNK, w), lambda b, s: (rb(False, b, s), 0))),
        out_shape=(jax.ShapeDtypeStruct((r, w), F32), jax.ShapeDtypeStruct((r, w), F32)),
        scratch_shapes=[pltpu.VMEM((2, HEADS, HEAD_DIM, HEAD_DIM), F32)],
        compiler_params=_cparams(("parallel", "arbitrary"), 40),
        name="gdn_scan",
    )(qkvn, qkvn, qkvn, small, abt3, qkvn, qkvn, qkvn, small, abt3, al, dtb, al.reshape(-1, 1), dtb.reshape(-1, 1))


def _gdn_out_kernel(of_ref, ob_ref, z_ref, g_ref, y_ref):
    o = of_ref[...] + ob_ref[...]
    z = z_ref[...]
    for h in range(HEADS):
        sl = slice(h * HEAD_DIM, (h + 1) * HEAD_DIM)
        seg = o[:, sl]
        ms = jnp.mean(seg * seg, -1, keepdims=True)
        y_ref[:, sl] = (seg * lax.rsqrt(ms + EPS) * g_ref[...] * _silu(z[:, sl])).astype(y_ref.dtype)


def _gdn_out(o_fwd, o_bwd, z, gdn_norm, m):
    w = QK_W
    return pl.pallas_call(
        _gdn_out_kernel,
        grid=(m // ROW_TILE,),
        in_specs=[pl.BlockSpec((ROW_TILE, w), lambda i: (i, 0)),
                  pl.BlockSpec((ROW_TILE, w), lambda i: (i, 0)),
                  pl.BlockSpec((ROW_TILE, w), lambda i: (i, 0)),
                  pl.BlockSpec((1, HEAD_DIM), lambda i: (0, 0))],
        out_specs=pl.BlockSpec((ROW_TILE, w), lambda i: (i, 0)),
        out_shape=jax.ShapeDtypeStruct((m, w), BF16),
        compiler_params=_cparams(("parallel",)),
        name="gdn_out",
    )(o_fwd, o_bwd, z, gdn_norm.reshape(1, HEAD_DIM))


ATTN_SUB = 256
Q_HEAD_W = 2 * HEAD_DIM
UQ_HEAD_COLS = 3 * HEAD_DIM


def _rms(x, g):
    return x * lax.rsqrt(jnp.mean(x * x, -1, keepdims=True) + EPS) * g


def _mla_q_kernel(x_ref, g_ref, w_ref, tab_ref, o_ref):
    xn = _rms(x_ref[...], g_ref[...]).astype(BF16)
    cos_s = tab_ref[:, 0:LANES]
    sin_s = tab_ref[:, LANES:2 * LANES]
    for h in range(HEADS):
        r = _dot(xn, w_ref[:, h * UQ_HEAD_COLS:(h + 1) * UQ_HEAD_COLS])
        o_ref[:, h * Q_HEAD_W:h * Q_HEAD_W + LANES] = (r[:, 0:LANES] * Q_SCALE).astype(o_ref.dtype)
        rope = r[:, LANES:2 * LANES] * cos_s + r[:, 2 * LANES:] * sin_s
        o_ref[:, h * Q_HEAD_W + LANES:(h + 1) * Q_HEAD_W] = rope.astype(o_ref.dtype)


def _mla_q(small, q_norm, w_ext, tab, rows, m):
    tm = ROPE_TILE
    return pl.pallas_call(
        _mla_q_kernel,
        grid=(m // tm,),
        in_specs=[pl.BlockSpec((tm, Q_LORA), lambda i: (i, 0)),
                  pl.BlockSpec((1, Q_LORA), lambda i: (0, 0)),
                  pl.BlockSpec(w_ext.shape, lambda i: (0, 0)),
                  _rope_spec(rows)],
        out_specs=pl.BlockSpec((tm, HEADS * Q_HEAD_W), lambda i: (i, 0)),
        out_shape=jax.ShapeDtypeStruct((m, HEADS * Q_HEAD_W), BF16),
        compiler_params=_cparams(("parallel",), 48),
        name="mla_q",
    )(small, q_norm.reshape(1, Q_LORA), w_ext, tab)


def _mla_kv_kernel(x_ref, kr_ref, g_ref, w_ref, tab_ref, k_ref, v_ref):
    xn = _rms(x_ref[...], g_ref[...]).astype(BF16)
    kr = kr_ref[...]
    rope = kr[:, 0:LANES] * tab_ref[:, 2 * LANES:3 * LANES] + kr[:, LANES:] * tab_ref[:, 3 * LANES:]
    rope = rope.astype(k_ref.dtype)
    for h in range(HEADS):
        r = _dot(xn, w_ref[:, h * Q_HEAD_W:(h + 1) * Q_HEAD_W])
        k_ref[:, h * Q_HEAD_W:h * Q_HEAD_W + LANES] = r[:, 0:LANES].astype(k_ref.dtype)
        k_ref[:, h * Q_HEAD_W + LANES:(h + 1) * Q_HEAD_W] = rope
        v_ref[:, h * HEAD_DIM:(h + 1) * HEAD_DIM] = r[:, LANES:].astype(v_ref.dtype)


def _mla_kv(small, kv_norm, w_ukv, tab, rows, m):
    tm = ROPE_TILE
    return pl.pallas_call(
        _mla_kv_kernel,
        grid=(m // tm,),
        in_specs=[pl.BlockSpec((tm, KV_LORA), lambda i: (i, 1)),
                  pl.BlockSpec((tm, 2 * LANES), lambda i: (i, 4)),
                  pl.BlockSpec((1, KV_LORA), lambda i: (0, 0)),
                  pl.BlockSpec(w_ukv.shape, lambda i: (0, 0)),
                  _rope_spec(rows)],
        out_specs=(pl.BlockSpec((tm, HEADS * Q_HEAD_W), lambda i: (i, 0)),
                   pl.BlockSpec((tm, HEADS * HEAD_DIM), lambda i: (i, 0))),
        out_shape=(jax.ShapeDtypeStruct((m, HEADS * Q_HEAD_W), BF16),
                   jax.ShapeDtypeStruct((m, HEADS * HEAD_DIM), BF16)),
        compiler_params=_cparams(("parallel",), 48),
        name="mla_kv",
    )(small, small, kv_norm.reshape(1, KV_LORA), w_ukv, tab)


def _attn_kernel(*refs, with_latent):
    if with_latent:
        q_ref, kl_ref, vl_ref, kc_ref, vc_ref, o_ref = refs
    else:
        q_ref, kc_ref, vc_ref, o_ref = refs
    subs = [slice(i, i + ATTN_SUB) for i in range(0, q_ref.shape[0], ATTN_SUB)]

    def scores(s):
        q = q_ref[s, :]
        return (_dot_nt(q, kc_ref[...]), _dot_nt(q, kl_ref[...]) if with_latent else None)

    def finish(s, sc, sl):
        m = jnp.max(sc, -1, keepdims=True)
        if with_latent:
            m = jnp.maximum(m, jnp.max(sl, -1, keepdims=True))
        pc = jnp.exp2(sc - m)
        den = jnp.sum(pc, -1, keepdims=True)
        acc = _dot(pc.astype(BF16), vc_ref[...])
        if with_latent:
            p_lat = jnp.exp2(sl - m)
            den = den + jnp.sum(p_lat, -1, keepdims=True)
            acc = acc + _dot(p_lat.astype(BF16), vl_ref[...])
        o_ref[s, :] = (acc / den).astype(o_ref.dtype)

    pending = None
    for s in subs:
        nxt = (s,) + scores(s)
        if pending is not None:
            finish(*pending)
        pending = nxt
    finish(*pending)


def _attention(qf, kf, vf, rows, latent_queries, tq):
    b, l, c = rows.b, rows.l, rows.c
    ctx0 = rows.t // c
    kc_spec = pl.BlockSpec((c, Q_HEAD_W), lambda bi, h, qi: (ctx0 + bi, h))
    vc_spec = pl.BlockSpec((c, HEAD_DIM), lambda bi, h, qi: (ctx0 + bi, h))
    if latent_queries:
        nq = l // tq
        q_spec = pl.BlockSpec((tq, Q_HEAD_W), lambda bi, h, qi: (bi * nq + qi, h))
        o_spec = pl.BlockSpec((tq, HEAD_DIM), lambda bi, h, qi: (bi * nq + qi, h))
        in_specs = [q_spec,
                    pl.BlockSpec((l, Q_HEAD_W), lambda bi, h, qi: (bi, h)),
                    pl.BlockSpec((l, HEAD_DIM), lambda bi, h, qi: (bi, h)),
                    kc_spec, vc_spec]
        args = (qf, kf, vf, kf, vf)
        m_out = rows.t
    else:
        nq = c // tq
        q0 = rows.t // tq
        q_spec = pl.BlockSpec((tq, Q_HEAD_W), lambda bi, h, qi: (q0 + bi * nq + qi, h))
        o_spec = pl.BlockSpec((tq, HEAD_DIM), lambda bi, h, qi: (bi * nq + qi, h))
        in_specs = [q_spec, kc_spec, vc_spec]
        args = (qf, kf, vf)
        m_out = rows.r - rows.t
    return pl.pallas_call(
        functools.partial(_attn_kernel, with_latent=latent_queries),
        grid=(b, HEADS, nq),
        in_specs=in_specs, out_specs=o_spec,
        out_shape=jax.ShapeDtypeStruct((m_out, HEADS * HEAD_DIM), BF16),
        compiler_params=_cparams(("parallel", "parallel", "arbitrary"), 48),
        name="mla_attn",
    )(*args)


def _router_kernel(h_ref, w_ref, ri_ref, pos_ref, cnt_ref, carry_ref, *, n_experts):
    i = pl.program_id(0)

    @pl.when(i == 0)
    def _():
        carry_ref[...] = jnp.zeros_like(carry_ref)

    logits = jnp.dot(h_ref[...], w_ref[...], preferred_element_type=F32, precision=HI)
    lane = lax.broadcasted_iota(jnp.int32, logits.shape, 1).astype(F32)
    neg = -jnp.inf
    logits = jnp.where(lane < n_experts, logits, neg)
    m1 = jnp.max(logits, -1, keepdims=True)
    i1 = jnp.min(jnp.where(logits == m1, lane, float(LANES)), -1, keepdims=True)
    sel1 = lane == i1
    rest = jnp.where(sel1, neg, logits)
    m2 = jnp.max(rest, -1, keepdims=True)
    i2 = jnp.min(jnp.where(rest == m2, lane, float(LANES)), -1, keepdims=True)
    sel2 = lane == i2
    e = jnp.exp(m2 - m1)
    w1 = 1.0 / (1.0 + e)
    w2 = e / (1.0 + e)
    ri_ref[...] = jnp.where(lane == 0, i1,
                            jnp.where(lane == 1, i2,
                                      jnp.where(lane == 2, w1, jnp.where(lane == 3, w2, 0.0))))
    onehot = jnp.where(sel1, 1.0, jnp.where(sel2, 1.0, 0.0))
    tile = logits.shape[0]
    ri_t = lax.broadcasted_iota(jnp.int32, (tile, tile), 0)
    ci_t = lax.broadcasted_iota(jnp.int32, (tile, tile), 1)
    before = (ci_t < ri_t).astype(BF16)
    pos_ref[...] = _dot(before, onehot.astype(BF16)) + carry_ref[0:1, :]
    carry_ref[0:1, :] = carry_ref[0:1, :] + jnp.sum(onehot, axis=0, keepdims=True)
    cnt_ref[...] = jnp.broadcast_to(carry_ref[0:1, :], cnt_ref.shape)


def _router(hf, router_w, n_experts):
    t, d = hf.shape
    wpad = jnp.zeros((d, LANES), F32).at[:, :n_experts].set(router_w)
    return pl.pallas_call(
        functools.partial(_router_kernel, n_experts=n_experts),
        grid=(t // ROW_TILE,),
        in_specs=[pl.BlockSpec((ROW_TILE, d), lambda i: (i, 0)),
                  pl.BlockSpec((d, LANES), lambda i: (0, 0))],
        out_specs=(pl.BlockSpec((ROW_TILE, LANES), lambda i: (i, 0)),
                   pl.BlockSpec((ROW_TILE, LANES), lambda i: (i, 0)),
                   pl.BlockSpec((8, LANES), lambda i: (0, 0))),
        out_shape=(jax.ShapeDtypeStruct((t, LANES), F32),
                   jax.ShapeDtypeStruct((t, LANES), F32),
                   jax.ShapeDtypeStruct((8, LANES), F32)),
        scratch_shapes=[pltpu.VMEM((8, LANES), F32)],
        compiler_params=_cparams(("arbitrary",)),
        name="moe_router",
    )(hf, wpad)


def _row_copy(src_hbm, row, buf, r, sem):
    return pltpu.make_async_copy(src_hbm.at[pl.ds(row, 1), :], buf.at[pl.ds(r, 1), :], sem)


GATHER_UNROLL = 8


def _gather_rows(idx_ref, base, n, src_hbm, buf, sem):
    def issue(it, c):
        for u in range(GATHER_UNROLL):
            r = it * GATHER_UNROLL + u
            _row_copy(src_hbm, idx_ref[base + r], buf, r, sem).start(priority=u % 2)
        return c

    lax.fori_loop(0, n // GATHER_UNROLL, issue, 0)
    pltpu.make_async_copy(src_hbm.at[pl.ds(0, n), :], buf, sem).wait()


def _moe_gather_kernel(valid_ref, tok_ref, h_hbm, o_ref, buf, sem):
    i = pl.program_id(0)

    @pl.when(valid_ref[i] > 0)
    def _():
        _gather_rows(tok_ref, i * MOE_TILE, MOE_TILE, h_hbm, buf, sem)
        o_ref[...] = buf[...].astype(o_ref.dtype)

    @pl.when(valid_ref[i] == 0)
    def _():
        o_ref[...] = jnp.zeros_like(o_ref)


def _moe_gather(valid, tok, hf, n_slots):
    d = hf.shape[1]
    return pl.pallas_call(
        _moe_gather_kernel,
        grid_spec=pltpu.PrefetchScalarGridSpec(
            num_scalar_prefetch=2,
            grid=(n_slots // MOE_TILE,),
            in_specs=[pl.BlockSpec(memory_space=pl.ANY)],
            out_specs=pl.BlockSpec((MOE_TILE, d), lambda i, v, t: (i, 0)),
            scratch_shapes=[pltpu.VMEM((MOE_TILE, d), F32), pltpu.SemaphoreType.DMA]),
        out_shape=jax.ShapeDtypeStruct((n_slots, d), BF16),
        compiler_params=_cparams(("arbitrary",)),
        name="moe_gather",
    )(valid, tok, hf)


def _moe_up_kernel(te_ref, valid_ref, x_ref, w1_ref, w3_ref, o_ref, w1b_ref, w3b_ref):
    i = pl.program_id(1)

    @pl.when((i == 0) | (te_ref[i] != te_ref[jnp.maximum(i - 1, 0)]))
    def _():
        w1b_ref[...] = w1_ref[...].astype(BF16)
        w3b_ref[...] = w3_ref[...].astype(BF16)

    @pl.when(valid_ref[i] > 0)
    def _():
        x = x_ref[...]
        o_ref[...] = (_silu(_dot(x, w1b_ref[...])) * _dot(x, w3b_ref[...])).astype(o_ref.dtype)

    @pl.when(valid_ref[i] == 0)
    def _():
        o_ref[...] = jnp.zeros_like(o_ref)


def _moe_up(te, valid, xs, w1, w3, layer, tf):
    s, d = xs.shape
    f = w1.shape[3]
    w_spec = pl.BlockSpec((None, None, d, tf), lambda j, i, te, v: (layer, te[i], 0, j))
    return pl.pallas_call(
        _moe_up_kernel,
        grid_spec=pltpu.PrefetchScalarGridSpec(
            num_scalar_prefetch=2,
            grid=(f // tf, s // MOE_TILE),
            in_specs=[pl.BlockSpec((MOE_TILE, d), lambda j, i, te, v: (i, 0)), w_spec, w_spec],
            out_specs=pl.BlockSpec((MOE_TILE, tf), lambda j, i, te, v: (i, j)),
            scratch_shapes=[pltpu.VMEM((d, tf), BF16), pltpu.VMEM((d, tf), BF16)]),
        out_shape=jax.ShapeDtypeStruct((s, f), BF16),
        compiler_params=_cparams(("arbitrary", "arbitrary"), 48),
        name="moe_up",
    )(te, valid, xs, w1, w3)


def _moe_down_kernel(te_ref, valid_ref, x_ref, w_ref, o_ref):
    i = pl.program_id(1)

    @pl.when(valid_ref[i] > 0)
    def _():
        o_ref[...] = _dot(x_ref[...], w_ref[...])

    @pl.when(valid_ref[i] == 0)
    def _():
        o_ref[...] = jnp.zeros_like(o_ref)


def _moe_down(te, valid, hmid, w2, tn):
    s, f = hmid.shape
    d = w2.shape[2]
    return pl.pallas_call(
        _moe_down_kernel,
        grid_spec=pltpu.PrefetchScalarGridSpec(
            num_scalar_prefetch=2,
            grid=(d // tn, s // MOE_TILE),
            in_specs=[pl.BlockSpec((MOE_TILE, f), lambda j, i, te, v: (i, 0)),
                      pl.BlockSpec((None, f, tn), lambda j, i, te, v: (te[i], 0, j))],
            out_specs=pl.BlockSpec((MOE_TILE, tn), lambda j, i, te, v: (i, j))),
        out_shape=jax.ShapeDtypeStruct((s, d), F32),
        compiler_params=_cparams(("arbitrary", "arbitrary"), 48),
        name="moe_down",
    )(te, valid, hmid, w2)


def _moe_combine_kernel(s1_ref, s2_ref, y_hbm, ri_ref, x_ref, gt_ref, g_ref, b_ref, o_ref,
                        buf1, buf2, sem1, sem2, *, alpha):
    i = pl.program_id(0)
    base = i * ROW_TILE

    def issue(it, c):
        for u in range(GATHER_UNROLL):
            r = it * GATHER_UNROLL + u
            _row_copy(y_hbm, s1_ref[base + r], buf1, r, sem1).start(priority=0)
            _row_copy(y_hbm, s2_ref[base + r], buf2, r, sem2).start(priority=1)
        return c

    lax.fori_loop(0, ROW_TILE // GATHER_UNROLL, issue, 0)
    pltpu.make_async_copy(y_hbm.at[pl.ds(0, ROW_TILE), :], buf1, sem1).wait()
    pltpu.make_async_copy(y_hbm.at[pl.ds(0, ROW_TILE), :], buf2, sem2).wait()
    ri = ri_ref[...]
    y = ri[:, 2:3] * buf1[...] + ri[:, 3:4] * buf2[...]
    xn = _layer_norm(alpha * x_ref[...] + gt_ref[...] * y)
    o_ref[...] = xn * g_ref[...] + b_ref[...]


def _moe_combine(slot1, slot2, yslot, ri, x, mods, gate_piece, ln_g, ln_b, rows, alpha):
    t, d = x.shape
    row_spec = pl.BlockSpec((ROW_TILE, d), lambda i, a, b: (i, 0))
    vec_spec = pl.BlockSpec((1, d), lambda i, a, b: (0, 0))
    return pl.pallas_call(
        functools.partial(_moe_combine_kernel, alpha=alpha),
        grid_spec=pltpu.PrefetchScalarGridSpec(
            num_scalar_prefetch=2,
            grid=(t // ROW_TILE,),
            in_specs=[pl.BlockSpec(memory_space=pl.ANY),
                      pl.BlockSpec((ROW_TILE, LANES), lambda i, a, b: (i, 0)),
                      row_spec,
                      pl.BlockSpec((None, None, 1, d), lambda i, a, b: (gate_piece, rows.group(i), 0, 0)),
                      vec_spec, vec_spec],
            out_specs=row_spec,
            scratch_shapes=[pltpu.VMEM((ROW_TILE, d), F32), pltpu.VMEM((ROW_TILE, d), F32),
                            pltpu.SemaphoreType.DMA, pltpu.SemaphoreType.DMA]),
        out_shape=jax.ShapeDtypeStruct((t, d), F32),
        compiler_params=_cparams(("arbitrary",)),
        name="moe_combine",
    )(slot1, slot2, yslot, ri, x, mods, ln_g, ln_b)


def _moe_plan(ri, pos, cnt, n_experts, n_slots):
    t = ri.shape[0]
    i1 = ri[:, 0].astype(jnp.int32)
    i2 = ri[:, 1].astype(jnp.int32)
    pos8 = pos[:, :n_experts].astype(jnp.int32)
    counts = cnt[0, :n_experts].astype(jnp.int32)
    padded = ((counts + MOE_TILE - 1) // MOE_TILE) * MOE_TILE
    ends = jnp.cumsum(padded)
    offs = ends - padded
    slot1 = offs[i1] + jnp.take_along_axis(pos8, i1[:, None], axis=1)[:, 0]
    slot2 = offs[i2] + jnp.take_along_axis(pos8, i2[:, None], axis=1)[:, 0]
    ids = jnp.arange(t, dtype=jnp.int32)
    tok = jnp.zeros((n_slots,), jnp.int32).at[slot1].set(ids).at[slot2].set(ids)
    starts = jnp.arange(n_slots // MOE_TILE, dtype=jnp.int32) * MOE_TILE
    valid = (starts < ends[-1]).astype(jnp.int32)
    te = jnp.minimum(jnp.searchsorted(ends, starts, side="right").astype(jnp.int32), n_experts - 1)
    te = jnp.where(valid > 0, te, jnp.max(jnp.where(valid > 0, te, 0)))
    return slot1, slot2, tok, te, valid


ROPE_TILE = 512


def _rope_tables(rows):
    l = rows.l
    nrow = l // GRID_W
    row = jnp.repeat(jnp.arange(nrow), GRID_W).astype(F32)
    col = jnp.tile(jnp.arange(GRID_W), nrow).astype(F32)
    n_freq = ROPE_DIM // 4
    inv_freq = ROPE_THETA ** (-jnp.arange(n_freq, dtype=F32) / n_freq)
    cr, sr = jnp.cos(row[:, None] * inv_freq), jnp.sin(row[:, None] * inv_freq)
    cc, sc = jnp.cos(col[:, None] * inv_freq), jnp.sin(col[:, None] * inv_freq)
    cos = jnp.concatenate([cr, cr, cc, cc], -1)
    sin = jnp.concatenate([-sr, sr, -sc, sc], -1)
    cos = jnp.concatenate([cos, jnp.ones((ROPE_TILE, ROPE_DIM), F32)], 0)
    sin = jnp.concatenate([sin, jnp.zeros((ROPE_TILE, ROPE_DIM), F32)], 0)
    z = jnp.zeros((l + ROPE_TILE, LANES - ROPE_DIM), F32)
    return jnp.concatenate([Q_SCALE * cos, z, Q_SCALE * sin, z, cos, z, sin, z], -1)


def _rope_spec(rows):
    lat_tiles, tps = rows.t // ROPE_TILE, rows.l // ROPE_TILE
    return pl.BlockSpec((ROPE_TILE, 4 * LANES), lambda i: (jnp.where(i < lat_tiles, i % tps, tps), 0))


_ROPE_SWAP = np.concatenate([np.arange(16, 32), np.arange(0, 16), np.arange(48, 64), np.arange(32, 48)])


def _uq_ext(w_uq):
    k = w_uq.shape[0]
    w = w_uq.reshape(k, HEADS, HEAD_DIM + ROPE_DIM)
    nope, rope = w[:, :, :HEAD_DIM], w[:, :, HEAD_DIM:]
    z = jnp.zeros((k, HEADS, LANES - ROPE_DIM), w.dtype)
    ext = jnp.concatenate([nope, rope, z, rope[:, :, _ROPE_SWAP], z], -1)
    return ext.reshape(k, HEADS * UQ_HEAD_COLS).astype(BF16)


def kernel(x, c, ctx, c_ctx, w_mod, b_mod, w_in, conv_w, a_log, dt_bias, gdn_norm, q_norm, kv_norm, w_uq, w_ukv, w_br_a, w_br_b, w_out, ln1_g, ln1_b, ln2_g, ln2_b, ffn_w1, ffn_w3, ffn_w2, moe_router, moe_w1, moe_w3, moe_w2):
    bsz, n_lat, d = x.shape
    n_ctx = ctx.shape[1]
    depth = w_mod.shape[0]
    assert depth == 2 and ffn_w1.shape[0] == 1 and moe_w1.shape[0] == 1
    rows = _Rows(bsz, n_lat, n_ctx)
    r_all, t_lat = rows.r, rows.t
    alpha = (2 * depth) ** 0.25
    n_experts = moe_router.shape[2]

    xs = jnp.concatenate([x.reshape(t_lat, d), ctx.reshape(r_all - t_lat, d)], 0)
    n_groups = -(-(bsz + 1) // 8) * 8
    cs = jnp.zeros((n_groups, d), F32).at[:bsz].set(c).at[bsz].set(c_ctx)
    tab = _rope_tables(rows)

    o_z = 3 * QK_W
    o_a = o_z + QK_W
    o_b = o_a + 2 * HEADS
    o_dq = o_b + 2 * HEADS
    o_dkv = o_dq + Q_LORA
    o_kr = o_dkv + KV_LORA
    o_ga = o_kr + ROPE_DIM
    ab_block = (Q_LORA + KV_LORA + 2 * LANES) // LANES

    tm_all = _pick(r_all, (1024, 512, 256))
    tm_lat = _pick(t_lat, (1024, 512, 256))
    tm_half = _pick(r_all, (512, 256))

    all_mods = [_mod_params(cs, w_mod, b_mod, i).reshape(n_groups, 6, d).transpose(1, 0, 2)[:, :, None, :]
                for i in range(depth)]

    h = None
    for i in range(depth):
        last = i == depth - 1
        m_mix = t_lat if last else r_all
        tm_mix = tm_lat if last else tm_all
        mods = all_mods[i]
        if i == 0:
            h = _ln_mod(xs, mods, rows, r_all, 0, 1, BF16)

        wi = w_in[i]
        kr_w = wi[:, o_kr:o_ga]
        zc = jnp.zeros((d, LANES - ROPE_DIM), F32)
        w_small = jnp.concatenate(
            [wi[:, o_dq:o_kr], kr_w, zc, kr_w[:, _ROPE_SWAP], zc, wi[:, o_a:o_dq], zc], 1).astype(BF16)
        qkv = _mm(h, wi[:, :o_z].astype(BF16), r_all, F32, tm_all, _pick(o_z, (1536, 1024, 512)))
        small = _mm(h, w_small, r_all, F32, tm_all, w_small.shape[1])
        z = _mm(h, wi[:, o_z:o_a].astype(BF16), m_mix, F32, tm_mix, 1024)
        gates = _mm(h, wi[:, o_ga:].astype(BF16), m_mix, F32, tm_mix, 1024)

        qkvn = _gdn_prep(qkv, conv_w[i], rows)
        ab = small[:, ab_block * LANES:ab_block * LANES + 4 * HEADS]
        abt3 = ab.reshape(r_all // CHUNK, CHUNK, 4 * HEADS).transpose(0, 2, 1)
        o_fwd, o_bwd = _gdn_scan(qkvn, small, abt3, a_log[i], dt_bias[i], rows, ab_block)
        ya = _gdn_out(o_fwd, o_bwd, z, gdn_norm[i], m_mix)

        qf = _mla_q(small, q_norm[i], _uq_ext(w_uq[i]), tab, rows, m_mix)
        kf, vf = _mla_kv(small, kv_norm[i], w_ukv[i].astype(BF16), tab, rows, r_all)
        yb = _attention(qf, kf, vf, rows, True, _pick(n_lat, (2048, 1024, 512, 256)))
        if not last:
            yb_ctx = _attention(qf, kf, vf, rows, False, _pick(n_ctx, (256,)))
            yb = jnp.concatenate([yb, yb_ctx], 0)

        ym = _merge(ya, yb, w_br_a[i].astype(BF16), w_br_b[i].astype(BF16), gates, m_mix,
                    _pick(m_mix, (512, 256)), 1024)
        mres = _mm(ym, w_out[i].astype(BF16), m_mix, F32, tm_mix, 1024)
        moe_layer = i % 2 == 1
        x1, h2 = _resid_ln(xs, mres, mods, 2, ln1_g[i].reshape(1, d), ln1_b[i].reshape(1, d), rows, m_mix,
                           alpha, next_mods=(mods, 3, 4), next_dtype=F32 if moe_layer else BF16)

        g2, b2 = ln2_g[i].reshape(1, d), ln2_b[i].reshape(1, d)
        if not moe_layer:
            j = i // 2
            hmid = _swiglu_up(h2, ffn_w1[j].astype(BF16), ffn_w3[j].astype(BF16), m_mix, tm_mix, 512)
            f_out = _mm(hmid, ffn_w2[j].astype(BF16), m_mix, F32, _pick(m_mix, (512, 256)), 1024, n_outer=True)
            if last:
                xs = _resid_ln(x1, f_out, mods, 5, g2, b2, rows, m_mix, alpha)
            else:
                xs, h = _resid_ln(x1, f_out, mods, 5, g2, b2, rows, m_mix, alpha,
                                  next_mods=(all_mods[i + 1], 0, 1))
        else:
            assert last
            j = i // 2
            ri, pos, cnt = _router(h2, moe_router[j], n_experts)
            n_slots = TOP_K * t_lat + n_experts * MOE_TILE
            slot1, slot2, tok, te, valid = _moe_plan(ri, pos, cnt, n_experts, n_slots)
            xg = _moe_gather(valid, tok, h2, n_slots)
            hmid = _moe_up(te, valid, xg, moe_w1, moe_w3, j, 512)
            yslot = _moe_down(te, valid, hmid, moe_w2[j].astype(BF16), 512)
            xs = _moe_combine(slot1, slot2, yslot, ri, x1, mods, 5, g2, b2, rows, alpha)
    return xs[:t_lat].reshape(bsz, n_lat, d)
```
